```python
import math
import jax, jax.numpy as jnp
from jax import lax
import numpy as np

D_MODEL = 4096
BATCH = 4
SEQ = 4096
DEPTH = 2

MIX_HALF = D_MODEL // 2
ROPE_THETA = 500000.0
Q_BLOCK = 128
EPS = 1e-6
N_EVEN = (DEPTH + 1) // 2
N_ODD = DEPTH // 2

A_QK_DIM = 128
A_V_DIM = 2 * A_QK_DIM
A_HEADS = MIX_HALF // A_V_DIM
A_ROT = A_QK_DIM // 4

B_V_DIM = 512
B_K_DIM = B_V_DIM // 2
B_HEADS = MIX_HALF // B_V_DIM
B_GATE_RANK = 16
B_GATE_TAU = 16.0
B_CHUNK = 64

C_NOPE = 128
C_ROPE = 64
C_V = 128
C_HEADS = MIX_HALF // C_V
C_Q_RANK = D_MODEL // 4
C_KV_RANK = 512

D_HEAD = 64
D_HEADS = MIX_HALF // D_HEAD
D_WIDTH = D_HEADS * D_HEAD
D_W_RANK = 96
D_A_RANK = 96
D_G_RANK = 256
D_GN_EPS = 64e-5

P_HEADS = 8
P_KEY_DIM = 256
P_HALF = P_KEY_DIM // 2
P_NKEYS = 128
P_N = P_NKEYS * P_NKEYS
P_TOPK = 16
P_TOKEN_BLOCK = 128

EVEN_SPLITS = (A_HEADS * 2 * A_QK_DIM, A_HEADS * 2 * A_QK_DIM, A_HEADS * A_V_DIM,
               B_HEADS * B_K_DIM, B_HEADS * B_K_DIM, B_HEADS * B_V_DIM,
               B_GATE_RANK, B_HEADS * B_V_DIM)
EVEN_IN = sum(EVEN_SPLITS)
EVEN_OUT = A_HEADS * A_V_DIM + B_HEADS * B_V_DIM
C_SPLITS = (C_Q_RANK, C_KV_RANK, C_ROPE)
D_SPLITS = (D_WIDTH, D_WIDTH, D_WIDTH, D_W_RANK, D_A_RANK, D_G_RANK)
C_COLS = sum(C_SPLITS)
D_COLS = sum(D_SPLITS)
ODD_IN = C_COLS + D_COLS
ODD_OUT = C_HEADS * C_V + D_WIDTH

kernel_name = "hybrid_diffattn_gla_mla_rwkv7_peer"


def split_cols(p, sizes):
    out, start = [], 0
    for n in sizes:
        out.append(p[..., start:start + n])
        start += n
    return out


def rms_norm(x, g, eps=EPS):
    xf = x.astype(jnp.float32)
    y = xf * lax.rsqrt(jnp.mean(xf * xf, axis=-1, keepdims=True) + eps)
    return (y * g.astype(jnp.float32)).astype(x.dtype)


def rope_cos_sin(seq, rot_dim):
    inv = 1.0 / (ROPE_THETA ** (jnp.arange(0, rot_dim, 2, dtype=jnp.float32) / rot_dim))
    ang = jnp.arange(seq, dtype=jnp.float32)[:, None] * inv[None, :]
    return jnp.cos(ang), jnp.sin(ang)


def apply_rope(x, cos, sin):
    half = cos.shape[-1]
    rd = 2 * half
    xf = x[..., :rd].astype(jnp.float32)
    x1, x2 = xf[..., :half], xf[..., half:]
    c, s = cos[:, None, :], sin[:, None, :]
    rot = jnp.concatenate([x1 * c - x2 * s, x2 * c + x1 * s], axis=-1).astype(x.dtype)
    return jnp.concatenate([rot, x[..., rd:]], axis=-1)


def causal_softmax(q_blk, k_pre, q_start, scale):
    s = jnp.einsum('bhqd,bhkd->bhqk', q_blk, k_pre).astype(jnp.float32) * scale
    q_pos = q_start + jnp.arange(q_blk.shape[2])
    k_pos = jnp.arange(k_pre.shape[2])
    s = jnp.where(k_pos[None, :] <= q_pos[:, None], s, -jnp.inf)
    return jax.nn.softmax(s, axis=-1)


def causal_block_sweep(block_fn, seq):
    return jnp.concatenate([block_fn(s, s + Q_BLOCK) for s in range(0, seq, Q_BLOCK)], axis=2)


def diff_attention(q, k, v, lam_params, subln_gain, lam_init):
    B_, S = q.shape[0], q.shape[1]
    t = lambda a: a.transpose(0, 2, 1, 3)
    q1, q2 = t(q[:, :, :, 0]), t(q[:, :, :, 1])
    k1, k2 = t(k[:, :, :, 0]), t(k[:, :, :, 1])
    vh = t(v)
    lp = lam_params.astype(jnp.float32)
    lam = jnp.exp(jnp.sum(lp[0] * lp[1])) - jnp.exp(jnp.sum(lp[2] * lp[3])) + lam_init
    scale = A_QK_DIM ** -0.5

    def block(s0, s1):
        p1 = causal_softmax(q1[:, :, s0:s1], k1[:, :, :s1], s0, scale)
        p2 = causal_softmax(q2[:, :, s0:s1], k2[:, :, :s1], s0, scale)
        return jnp.einsum('bhqk,bhkv->bhqv', (p1 - lam * p2).astype(vh.dtype), vh[:, :, :s1])

    o = causal_block_sweep(block, S)
    o = rms_norm(o, subln_gain) * (1.0 - lam_init)
    return o.transpose(0, 2, 1, 3).reshape(B_, S, A_HEADS * A_V_DIM)


def gla_chunked(q, k, v, log_a):
    B_, S, H, dk = q.shape
    dv = v.shape[-1]
    nc = S // B_CHUNK

    def to_chunks(t):
        return t.astype(jnp.float32).reshape(B_, nc, B_CHUNK, H, t.shape[-1]).transpose(1, 0, 3, 2, 4)

    qc, kc, vc, gc = (to_chunks(t) for t in (q * (dk ** -0.5), k, v, log_a))
    mask = jnp.tril(jnp.ones((B_CHUNK, B_CHUNK), dtype=bool))

    def step(state, inp):
        qi, ki, vi, gi = inp
        b = jnp.cumsum(gi, axis=2)
        o_inter = jnp.einsum('bhtk,bhkv->bhtv', qi * jnp.exp(b), state)
        diff = jnp.where(mask[:, :, None], b[:, :, :, None, :] - b[:, :, None, :, :], -jnp.inf)
        attn = jnp.einsum('bhtk,bhsk,bhtsk->bhts', qi, ki, jnp.exp(diff))
        o_intra = jnp.einsum('bhts,bhsv->bhtv', attn, vi)
        b_last = b[:, :, -1:, :]
        state = state * jnp.exp(b_last[:, :, 0, :])[..., None] + jnp.einsum(
            'bhsk,bhsv->bhkv', ki * jnp.exp(b_last - b), vi)
        return state, o_inter + o_intra

    _, o = lax.scan(step, jnp.zeros((B_, H, dk, dv), jnp.float32), (qc, kc, vc, gc))
    return o.transpose(1, 0, 3, 2, 4).reshape(B_, S, H, dv)


def mla_attention(c_q, c_kv, k_rope_raw, q_norm, kv_norm, w_uq, w_ukv, cos, sin):
    B_, S = c_q.shape[0], c_q.shape[1]
    q = (rms_norm(c_q, q_norm) @ w_uq).reshape(B_, S, C_HEADS, C_NOPE + C_ROPE)
    q = jnp.concatenate([q[..., :C_NOPE], apply_rope(q[..., C_NOPE:], cos, sin)], axis=-1)
    kv = (rms_norm(c_kv, kv_norm) @ w_ukv).reshape(B_, S, C_HEADS, C_NOPE + C_V)
    k_rope = apply_rope(k_rope_raw[:, :, None, :], cos, sin)
    k = jnp.concatenate([kv[..., :C_NOPE], jnp.broadcast_to(k_rope, (B_, S, C_HEADS, C_ROPE))], axis=-1)
    v = kv[..., C_NOPE:]
    q, k, v = (t.transpose(0, 2, 1, 3) for t in (q, k, v))
    scale = (C_NOPE + C_ROPE) ** -0.5

    def block(s0, s1):
        p = causal_softmax(q[:, :, s0:s1], k[:, :, :s1], s0, scale)
        return jnp.einsum('bhqk,bhkv->bhqv', p.astype(v.dtype), v[:, :, :s1])

    o = causal_block_sweep(block, S)
    return o.transpose(0, 2, 1, 3).reshape(B_, S, C_HEADS * C_V)


def rwkv7_time_mix(r, k, v, xw, xa, xg, w0, w2, a0, a2, g2, k_k, k_a, r_k, gn_w, gn_b):
    B_, S, C = r.shape
    f32 = jnp.float32
    w = -jax.nn.softplus(-(w0 + jnp.tanh(xw) @ w2).astype(f32)) - 0.5
    a = jax.nn.sigmoid((a0 + xa @ a2).astype(f32))
    g = jax.nn.sigmoid(xg) @ g2
    heads = lambda t: t.astype(f32).reshape(B_, S, D_HEADS, D_HEAD)
    kk = heads(k * k_k)
    kk = kk * lax.rsqrt(jnp.maximum(jnp.sum(kk * kk, axis=-1, keepdims=True), 1e-24))
    k_eff = heads(k.astype(f32) * (1.0 + (a - 1.0) * k_a.astype(f32)))
    r_h, v_h, a_h = heads(r), heads(v), heads(a)
    decay = jnp.exp(-jnp.exp(heads(w)))

    def step(state, inp):
        r_t, d_t, k_t, v_t, a_t, b_t = inp
        sa = jnp.einsum('bhvk,bhk->bhv', state, a_t)
        state = (state * d_t[:, :, None, :] + sa[..., None] * b_t[:, :, None, :]
                 + v_t[..., None] * k_t[:, :, None, :])
        return state, jnp.einsum('bhvk,bhk->bhv', state, r_t)

    xs = tuple(jnp.moveaxis(t, 1, 0) for t in (r_h, decay, k_eff, v_h, -kk, kk * a_h))
    _, y = lax.scan(step, jnp.zeros((B_, D_HEADS, D_HEAD, D_HEAD), f32), xs)
    y = jnp.moveaxis(y, 0, 1)
    mu = jnp.mean(y, axis=-1, keepdims=True)
    var = jnp.mean(jnp.square(y - mu), axis=-1, keepdims=True)
    y = ((y - mu) * lax.rsqrt(var + D_GN_EPS)).reshape(B_, S, C) * gn_w.astype(f32) + gn_b.astype(f32)
    bonus = jnp.sum(r_h * k_eff * r_k.astype(f32), axis=-1, keepdims=True) * v_h
    return ((y + bonus.reshape(B_, S, C)) * g.astype(f32)).astype(r.dtype)


def even_mixer(h, w_in, w_out, lam_params, subln_gain, lam_init, gla_w_gate, gla_b_gate, gla_norm, cos, sin):
    B_, S, _ = h.shape
    p = h @ w_in
    aq, ak, av, bq, bk, bv, b_lo, b_og = split_cols(p, EVEN_SPLITS)
    aq = apply_rope(aq.reshape(B_, S, 2 * A_HEADS, A_QK_DIM), cos, sin).reshape(B_, S, A_HEADS, 2, A_QK_DIM)
    ak = apply_rope(ak.reshape(B_, S, 2 * A_HEADS, A_QK_DIM), cos, sin).reshape(B_, S, A_HEADS, 2, A_QK_DIM)
    o_a = diff_attention(aq, ak, av.reshape(B_, S, A_HEADS, A_V_DIM), lam_params, subln_gain, lam_init)
    log_a = jax.nn.log_sigmoid((b_lo @ gla_w_gate + gla_b_gate).astype(jnp.float32)) / B_GATE_TAU
    o_b = gla_chunked(bq.reshape(B_, S, B_HEADS, B_K_DIM), bk.reshape(B_, S, B_HEADS, B_K_DIM),
                      bv.reshape(B_, S, B_HEADS, B_V_DIM), log_a.reshape(B_, S, B_HEADS, B_K_DIM))
    o_b = rms_norm(o_b, gla_norm) * jax.nn.silu(b_og.reshape(B_, S, B_HEADS, B_V_DIM).astype(jnp.float32))
    o = jnp.concatenate([o_a, o_b.reshape(B_, S, B_HEADS * B_V_DIM).astype(h.dtype)], axis=-1)
    return o @ w_out


def odd_mixer(h, w_in, w_out, q_norm, kv_norm, w_uq, w_ukv, mu, w0, w2, a0, a2, g2,
              k_k, k_a, r_k, gn_w, gn_b, cos, sin):
    p = h @ w_in
    c_q, c_kv, k_rope = split_cols(p[..., :C_COLS], C_SPLITS)
    pd = p[..., C_COLS:]
    shifted = jnp.pad(pd, ((0, 0), (1, 0), (0, 0)))[:, :-1]
    pd = pd + (shifted - pd) * mu
    xr, xk, xv, xw, xa, xg = split_cols(pd, D_SPLITS)
    o_c = mla_attention(c_q, c_kv, k_rope, q_norm, kv_norm, w_uq, w_ukv, cos, sin)
    o_d = rwkv7_time_mix(xr, xk, xv, xw, xa, xg, w0, w2, a0, a2, g2, k_k, k_a, r_k, gn_w, gn_b)
    return jnp.concatenate([o_c, o_d], axis=-1) @ w_out


def peer_ffn(h, w_q, sub_keys, u, v):
    B_, S, Dm = h.shape
    T = B_ * S
    hf = h.reshape(T, Dm)
    q = (hf @ w_q).reshape(T, P_HEADS, 2, P_HALF)
    s = jnp.einsum('thjc,hjnc->thjn', q, sub_keys).astype(jnp.float32)
    top_s, top_i = lax.top_k(s, P_TOPK)
    cand = (top_s[:, :, 0, :, None] + top_s[:, :, 1, None, :]).reshape(T, P_HEADS, P_TOPK * P_TOPK)
    best_s, best_c = lax.top_k(cand, P_TOPK)
    idx = (jnp.take_along_axis(top_i[:, :, 0], best_c // P_TOPK, axis=-1) * P_NKEYS
           + jnp.take_along_axis(top_i[:, :, 1], best_c % P_TOPK, axis=-1))
    gate = jax.nn.softmax(best_s, axis=-1)
    nb = T // P_TOKEN_BLOCK

    def block(args):
        xb, ib, gb = args
        u_sel = jnp.take(u, ib, axis=0)
        act = jax.nn.gelu(jnp.einsum('td,thkd->thk', xb, u_sel).astype(jnp.float32), approximate=False)
        coef = (gb * act).astype(xb.dtype)
        return jnp.einsum('thk,thkd->td', coef, jnp.take(v, ib, axis=0))

    y = lax.map(block, (hf.reshape(nb, P_TOKEN_BLOCK, Dm),
                        idx.reshape(nb, P_TOKEN_BLOCK, P_HEADS, P_TOPK),
                        gate.reshape(nb, P_TOKEN_BLOCK, P_HEADS, P_TOPK)))
    return y.reshape(B_, S, Dm)


def setup_inputs(seed: int = 0) -> dict:
    key = jax.random.key(seed)
    ks = iter(jax.random.split(key, 40))
    f32 = jnp.float32
    nrm = lambda shape, scale: jax.random.normal(next(ks), shape, f32) * scale
    gain = lambda shape: 1.0 + 0.02 * jax.random.normal(next(ks), shape, f32)
    unif = lambda shape, lo, hi: jax.random.uniform(next(ks), shape, f32, lo, hi)
    return {
        "x": nrm((BATCH, SEQ, D_MODEL), 1.0),
        "norm_mix": gain((DEPTH, D_MODEL)),
        "norm_ffn": gain((DEPTH, D_MODEL)),
        "norm_final": gain((D_MODEL,)),
        "even_w_in": nrm((N_EVEN, D_MODEL, EVEN_IN), D_MODEL ** -0.5),
        "even_w_out": nrm((N_EVEN, EVEN_OUT, D_MODEL), EVEN_OUT ** -0.5),
        "diff_lambda": nrm((N_EVEN, 4, A_QK_DIM), 0.1),
        "diff_subln": gain((N_EVEN, A_V_DIM)),
        "gla_w_gate": nrm((N_EVEN, B_GATE_RANK, B_HEADS * B_K_DIM), B_GATE_RANK ** -0.5),
        "gla_b_gate": nrm((N_EVEN, B_HEADS * B_K_DIM), 0.1),
        "gla_norm": gain((N_EVEN, B_V_DIM)),
        "odd_w_in": nrm((N_ODD, D_MODEL, ODD_IN), D_MODEL ** -0.5),
        "odd_w_out": nrm((N_ODD, ODD_OUT, D_MODEL), ODD_OUT ** -0.5),
        "mla_q_norm": gain((N_ODD, C_Q_RANK)),
        "mla_kv_norm": gain((N_ODD, C_KV_RANK)),
        "mla_w_uq": nrm((N_ODD, C_Q_RANK, C_HEADS * (C_NOPE + C_ROPE)), C_Q_RANK ** -0.5),
        "mla_w_ukv": nrm((N_ODD, C_KV_RANK, C_HEADS * (C_NOPE + C_V)), C_KV_RANK ** -0.5),
        "rwkv_mu": unif((N_ODD, D_COLS), 0.0, 1.0),
        "rwkv_w0": unif((N_ODD, D_WIDTH), -7.0, -1.0),
        "rwkv_w2": nrm((N_ODD, D_W_RANK, D_WIDTH), 0.5 * D_W_RANK ** -0.5),
        "rwkv_a0": nrm((N_ODD, D_WIDTH), 0.1),
        "rwkv_a2": nrm((N_ODD, D_A_RANK, D_WIDTH), D_A_RANK ** -0.5),
        "rwkv_g2": nrm((N_ODD, D_G_RANK, D_WIDTH), D_G_RANK ** -0.5),
        "rwkv_k_k": 0.85 + 0.05 * jax.random.normal(next(ks), (N_ODD, D_WIDTH), f32),
        "rwkv_k_a": gain((N_ODD, D_WIDTH)),
        "rwkv_r_k": nrm((N_ODD, D_HEADS, D_HEAD), 0.1),
        "rwkv_gn_w": gain((N_ODD, D_WIDTH)),
        "rwkv_gn_b": nrm((N_ODD, D_WIDTH), 0.02),
        "peer_w_q": nrm((DEPTH, D_MODEL, P_HEADS * P_KEY_DIM), D_MODEL ** -0.5),
        "peer_keys": nrm((DEPTH, P_HEADS, 2, P_NKEYS, P_HALF), P_HALF ** -0.5),
        "peer_u": nrm((DEPTH, P_N, D_MODEL), D_MODEL ** -0.5),
        "peer_v": nrm((DEPTH, P_N, D_MODEL), P_HEADS ** -0.5),
    }


def reference(x, norm_mix, norm_ffn, norm_final,
              even_w_in, even_w_out, diff_lambda, diff_subln, gla_w_gate, gla_b_gate, gla_norm,
              odd_w_in, odd_w_out, mla_q_norm, mla_kv_norm, mla_w_uq, mla_w_ukv,
              rwkv_mu, rwkv_w0, rwkv_w2, rwkv_a0, rwkv_a2, rwkv_g2, rwkv_k_k, rwkv_k_a, rwkv_r_k,
              rwkv_gn_w, rwkv_gn_b,
              peer_w_q, peer_keys, peer_u, peer_v):
    S = x.shape[1]
    cos_a, sin_a = rope_cos_sin(S, A_ROT)
    cos_c, sin_c = rope_cos_sin(S, C_ROPE)
    h = x
    for layer in range(DEPTH):
        j = layer // 2
        hn = rms_norm(h, norm_mix[layer])
        if layer % 2 == 0:
            lam_init = 0.8 - 0.6 * math.exp(-0.3 * layer)
            mix = even_mixer(hn, even_w_in[j], even_w_out[j], diff_lambda[j], diff_subln[j], lam_init,
                             gla_w_gate[j], gla_b_gate[j], gla_norm[j], cos_a, sin_a)
        else:
            mix = odd_mixer(hn, odd_w_in[j], odd_w_out[j], mla_q_norm[j], mla_kv_norm[j], mla_w_uq[j],
                            mla_w_ukv[j], rwkv_mu[j], rwkv_w0[j], rwkv_w2[j], rwkv_a0[j], rwkv_a2[j],
                            rwkv_g2[j], rwkv_k_k[j], rwkv_k_a[j], rwkv_r_k[j], rwkv_gn_w[j], rwkv_gn_b[j],
                            cos_c, sin_c)
        h = h + mix
        h = h + peer_ffn(rms_norm(h, norm_ffn[layer]), peer_w_q[layer], peer_keys[layer],
                         peer_u[layer], peer_v[layer])
    return rms_norm(h, norm_final)
```

```python
import functools
import math

import jax
import jax.numpy as jnp
from jax import lax
from jax.experimental import pallas as pl
from jax.experimental.pallas import tpu as pltpu

F32 = jnp.float32
BF16 = jnp.bfloat16
HIGHEST = lax.Precision.HIGHEST

LANES = 128
VMEM_LIMIT = 56 * 1024 * 1024

EPS = 1e-6
ROPE_THETA = 500000.0
A_QK_DIM = 128
A_V_DIM = 256
A_ROT = A_QK_DIM // 4
B_V_DIM = 512
B_K_DIM = 256
B_GATE_TAU = 16.0
GLA_CHUNK = 64
C_NOPE = 128
C_ROPE = 64
C_V = 128
D_HEAD = 64
D_GN_EPS = 64e-5
RWKV_CHUNK = 64
RWKV_HEADS_PER_STEP = 4
P_HEADS = 8
P_NKEYS = 128
P_TOPK = 16

NT_DIMS = (((1,), (1,)), ((), ()))
TN_DIMS = (((0,), (0,)), ((), ()))


def _params(*sem):
    return pltpu.CompilerParams(dimension_semantics=sem, vmem_limit_bytes=VMEM_LIMIT)


def _pick(n, cands):
    for c in cands:
        if n % c == 0:
            return c
    raise ValueError(f"no tile in {cands} divides {n}")


def _pad_cols(w, n):
    return jnp.pad(w, ((0, 0), (0, n - w.shape[1])))


def _norm_kernel(*refs, eps, n_add, emit_sum):
    xs = refs[:n_add]
    g_ref = refs[n_add]
    outs = refs[n_add + 1:]
    x = xs[0][...].astype(F32)
    for r in xs[1:]:
        x = x + r[...].astype(F32)
    y = x * lax.rsqrt(jnp.mean(x * x, axis=-1, keepdims=True) + eps) * g_ref[...]
    if emit_sum:
        outs[0][...] = x
        outs[1][...] = y.astype(outs[1].dtype)
    else:
        outs[0][...] = y.astype(outs[0].dtype)


def add_rmsnorm(xs, g, out_dtype, *, emit_sum=False, width=None, col_block=0, eps=EPS):
    t = xs[0].shape[0]
    width = width or xs[0].shape[1]
    tr = _pick(t, (128, 64, 32, 16, 8))
    spec = pl.BlockSpec((tr, width), lambda i: (i, col_block))
    ospec = pl.BlockSpec((tr, width), lambda i: (i, 0))
    out_shape = [jax.ShapeDtypeStruct((t, width), out_dtype)]
    out_specs = [ospec]
    if emit_sum:
        out_shape = [jax.ShapeDtypeStruct((t, width), F32)] + out_shape
        out_specs = [ospec, ospec]
    res = pl.pallas_call(
        functools.partial(_norm_kernel, eps=eps, n_add=len(xs), emit_sum=emit_sum),
        grid=(t // tr,),
        in_specs=[spec] * len(xs) + [pl.BlockSpec((1, width), lambda i: (0, 0))],
        out_specs=out_specs,
        out_shape=out_shape,
        compiler_params=_params("parallel"),
        name="add_rmsnorm",
    )(*xs, g.reshape(1, width).astype(F32))
    return res if emit_sum else res[0]


def _mm_kernel(*refs, n_in):
    o_ref = refs[2 * n_in]
    acc = jnp.dot(refs[0][...], refs[n_in][...], preferred_element_type=F32)
    for i in range(1, n_in):
        acc = acc + jnp.dot(refs[i][...], refs[n_in + i][...], preferred_element_type=F32)
    o_ref[...] = acc.astype(o_ref.dtype)


def matmul(xs, ws, out_dtype):
    m = xs[0].shape[0]
    n = ws[0].shape[1]
    tm = _pick(m, (1024, 512, 256, 128, 64, 32, 16))
    tn = _pick(n, (512, 768, 640, 256, 128))
    in_specs = [pl.BlockSpec((tm, x.shape[1]), lambda i, j: (i, 0)) for x in xs]
    in_specs += [pl.BlockSpec((w.shape[0], tn), lambda i, j: (0, j)) for w in ws]
    return pl.pallas_call(
        functools.partial(_mm_kernel, n_in=len(xs)),
        grid=(m // tm, n // tn),
        in_specs=in_specs,
        out_specs=pl.BlockSpec((tm, tn), lambda i, j: (i, j)),
        out_shape=jax.ShapeDtypeStruct((m, n), out_dtype),
        compiler_params=_params("parallel", "arbitrary"),
        name="matmul",
    )(*xs, *ws)


def rope_tables(seq, rot_dim):
    half = rot_dim // 2
    inv = 1.0 / (ROPE_THETA ** (jnp.arange(0, rot_dim, 2, dtype=F32) / rot_dim))
    ang = jnp.arange(seq, dtype=F32)[:, None] * inv[None, :]
    cos, sin = jnp.cos(ang), jnp.sin(ang)
    z = jnp.zeros((seq, LANES - 2 * half), F32)
    zh = jnp.zeros((seq, half), F32)
    c = jnp.concatenate([cos, cos, jnp.ones_like(z)], axis=1)
    s_up = jnp.concatenate([zh, sin, z], axis=1)
    s_dn = jnp.concatenate([-sin, zh, z], axis=1)
    return c, s_up, s_dn


def _rope_kernel(x_ref, c_ref, su_ref, sd_ref, o_ref, *, half, groups):
    c, su, sd = c_ref[...], su_ref[...], sd_ref[...]
    for g in range(groups):
        sl = slice(g * LANES, (g + 1) * LANES)
        x = x_ref[:, sl].astype(F32)
        y = x * c + pltpu.roll(x, half, 1) * su + pltpu.roll(x, LANES - half, 1) * sd
        o_ref[:, sl] = y.astype(o_ref.dtype)


def rope(x, tables, seq, *, width, col_block, half):
    t = x.shape[0]
    tr = _pick(seq, (256, 128, 64, 32, 16))
    nb = seq // tr
    tspec = pl.BlockSpec((tr, LANES), lambda i: (i % nb, 0))
    return pl.pallas_call(
        functools.partial(_rope_kernel, half=half, groups=width // LANES),
        grid=(t // tr,),
        in_specs=[pl.BlockSpec((tr, width), lambda i: (i, col_block)), tspec, tspec, tspec],
        out_specs=pl.BlockSpec((tr, width), lambda i: (i, 0)),
        out_shape=jax.ShapeDtypeStruct((t, width), BF16),
        compiler_params=_params("parallel"),
        name="rope",
    )(x, *tables)


def _flash_update(q, k, v, m_ref, l_ref, acc_ref, row0, col0, scale):
    s = lax.dot_general(q, k, NT_DIMS, preferred_element_type=F32) * scale
    rows = row0 + lax.broadcasted_iota(jnp.int32, s.shape, 0)
    cols = col0 + lax.broadcasted_iota(jnp.int32, s.shape, 1)
    s = jnp.where(cols <= rows, s, -jnp.inf)
    m_prev = m_ref[...]
    m_new = jnp.maximum(m_prev, jnp.max(s, axis=-1, keepdims=True))
    alpha = jnp.exp(m_prev - m_new)
    p = jnp.exp(s - m_new)
    l_ref[...] = alpha * l_ref[...] + jnp.sum(p, axis=-1, keepdims=True)
    acc_ref[...] = alpha * acc_ref[...] + jnp.dot(p.astype(v.dtype), v, preferred_element_type=F32)
    m_ref[...] = m_new


def _diff_attn_kernel(lam_ref, q_ref, k_ref, v_ref, g_ref, o_ref, m_sc, l_sc, acc_sc, *, scale, tq, out_scale, eps):
    i, j = pl.program_id(2), pl.program_id(3)

    @pl.when(j == 0)
    def _():
        m_sc[...] = jnp.full(m_sc.shape, -jnp.inf, F32)
        l_sc[...] = jnp.zeros(l_sc.shape, F32)
        acc_sc[...] = jnp.zeros(acc_sc.shape, F32)

    @pl.when(j <= i)
    def _():
        v = v_ref[...]
        for mp in range(2):
            sl = slice(mp * A_QK_DIM, (mp + 1) * A_QK_DIM)
            _flash_update(q_ref[:, sl], k_ref[:, sl], v, m_sc.at[mp], l_sc.at[mp], acc_sc.at[mp],
                          i * tq, j * tq, scale)

    @pl.when(j == pl.num_programs(3) - 1)
    def _():
        o = acc_sc[0] / l_sc[0] - lam_ref[0] * (acc_sc[1] / l_sc[1])
        o = o * lax.rsqrt(jnp.mean(o * o, axis=-1, keepdims=True) + eps) * g_ref[...] * out_scale
        o_ref[...] = o.astype(o_ref.dtype)


def diff_attention(qk, p, lam, subln_gain, lam_init, batch, seq, v_col):
    t = qk.shape[0]
    heads = qk.shape[1] // (4 * A_QK_DIM)
    hw = 2 * A_QK_DIM
    tq = _pick(seq, (512, 256, 128))
    nq = seq // tq
    vb = v_col // A_V_DIM
    kern = functools.partial(_diff_attn_kernel, scale=A_QK_DIM ** -0.5, tq=tq, out_scale=1.0 - lam_init, eps=EPS)
    return pl.pallas_call(
        kern,
        grid=(batch, heads, nq, nq),
        in_specs=[
            pl.BlockSpec(memory_space=pltpu.SMEM),
            pl.BlockSpec((tq, hw), lambda b, h, i, j: (b * nq + i, h)),
            pl.BlockSpec((tq, hw), lambda b, h, i, j: (b * nq + jnp.minimum(j, i), heads + h)),
            pl.BlockSpec((tq, A_V_DIM), lambda b, h, i, j: (b * nq + jnp.minimum(j, i), vb + h)),
            pl.BlockSpec((1, A_V_DIM), lambda b, h, i, j: (0, 0)),
        ],
        out_specs=pl.BlockSpec((tq, A_V_DIM), lambda b, h, i, j: (b * nq + i, h)),
        out_shape=jax.ShapeDtypeStruct((t, heads * A_V_DIM), BF16),
        scratch_shapes=[pltpu.VMEM((2, tq, 1), F32), pltpu.VMEM((2, tq, 1), F32), pltpu.VMEM((2, tq, A_V_DIM), F32)],
        compiler_params=_params("parallel", "parallel", "parallel", "arbitrary"),
        name="diff_attention",
    )(lam.reshape(1).astype(F32), qk, qk, p, subln_gain.reshape(1, A_V_DIM).astype(F32))


def _mla_attn_kernel(qn_ref, qr_ref, kn_ref, kr_ref, v_ref, o_ref, m_sc, l_sc, acc_sc, *, scale, tq):
    i, j = pl.program_id(2), pl.program_id(3)

    @pl.when(j == 0)
    def _():
        m_sc[...] = jnp.full(m_sc.shape, -jnp.inf, F32)
        l_sc[...] = jnp.zeros(l_sc.shape, F32)
        acc_sc[...] = jnp.zeros(acc_sc.shape, F32)

    @pl.when(j <= i)
    def _():
        q = jnp.concatenate([qn_ref[...], qr_ref[...]], axis=-1)
        k = jnp.concatenate([kn_ref[...], kr_ref[...]], axis=-1)
        _flash_update(q, k, v_ref[...], m_sc, l_sc, acc_sc, i * tq, j * tq, scale)

    @pl.when(j == pl.num_programs(3) - 1)
    def _():
        o_ref[...] = (acc_sc[...] / l_sc[...]).astype(o_ref.dtype)


def mla_attention(q_all, q_rope, kv, k_rope, batch, seq):
    t = kv.shape[0]
    heads = kv.shape[1] // (C_NOPE + C_V)
    tq = _pick(seq, (512, 256, 128))
    nq = seq // tq
    qmap = lambda b, h, i, j: (b * nq + i, h)
    kmap = lambda b, h, i, j: (b * nq + jnp.minimum(j, i), h)
    return pl.pallas_call(
        functools.partial(_mla_attn_kernel, scale=(C_NOPE + C_ROPE) ** -0.5, tq=tq),
        grid=(batch, heads, nq, nq),
        in_specs=[
            pl.BlockSpec((tq, C_NOPE), qmap),
            pl.BlockSpec((tq, LANES), qmap),
            pl.BlockSpec((tq, C_NOPE), kmap),
            pl.BlockSpec((tq, LANES), lambda b, h, i, j: (b * nq + jnp.minimum(j, i), 0)),
            pl.BlockSpec((tq, C_V), lambda b, h, i, j: (b * nq + jnp.minimum(j, i), heads + h)),
        ],
        out_specs=pl.BlockSpec((tq, C_V), qmap),
        out_shape=jax.ShapeDtypeStruct((t, heads * C_V), BF16),
        scratch_shapes=[pltpu.VMEM((tq, 1), F32), pltpu.VMEM((tq, 1), F32), pltpu.VMEM((tq, C_V), F32)],
        compiler_params=_params("parallel", "parallel", "parallel", "arbitrary"),
        name="mla_attention",
    )(q_all, q_rope, kv, k_rope, kv)


def _tril(n, strict=False):
    r = lax.broadcasted_iota(jnp.int32, (n, n), 0)
    c = lax.broadcasted_iota(jnp.int32, (n, n), 1)
    return (r > c) if strict else (r >= c)


def _gla_kernel(q_ref, k_ref, v_ref, og_ref, lo_ref, wg_ref, bg_ref, gn_ref, o_ref, st_sc, *, scale, eps):
    @pl.when(pl.program_id(2) == 0)
    def _():
        st_sc[...] = jnp.zeros(st_sc.shape, F32)

    n = q_ref.shape[0]
    z = jnp.dot(lo_ref[...], wg_ref[...], precision=HIGHEST, preferred_element_type=F32) + bg_ref[...]
    log_a = (jnp.minimum(z, 0.0) - jnp.log1p(jnp.exp(-jnp.abs(z)))) / B_GATE_TAU
    tri = _tril(n)
    b = jnp.dot(tri.astype(F32), log_a, precision=HIGHEST, preferred_element_type=F32)
    b_last = b[n - 1:n, :]
    q = q_ref[...].astype(F32) * scale
    k = k_ref[...].astype(F32)
    v = v_ref[...]
    qe = (q * jnp.exp(b)).astype(BF16)
    ke = (k * jnp.exp(-b)).astype(BF16)
    kl = (k * jnp.exp(b_last - b)).astype(BF16)
    st = st_sc[...]
    o = lax.dot_general(qe, st.astype(BF16), NT_DIMS, preferred_element_type=F32)
    attn = lax.dot_general(qe, ke, NT_DIMS, preferred_element_type=F32)
    attn = jnp.where(tri, attn, 0.0).astype(BF16)
    o = o + jnp.dot(attn, v, preferred_element_type=F32)
    st_sc[...] = st * jnp.exp(b_last) + lax.dot_general(v, kl, TN_DIMS, preferred_element_type=F32)
    o = o * lax.rsqrt(jnp.mean(o * o, axis=-1, keepdims=True) + eps) * gn_ref[...]
    og = og_ref[...].astype(F32)
    o_ref[...] = (o * (og * jax.nn.sigmoid(og))).astype(o_ref.dtype)


def gla(p, p_lo, w_gate, b_gate, gn, batch, seq, cols):
    t = p.shape[0]
    heads = w_gate.shape[1] // B_K_DIM
    ch = GLA_CHUNK
    nc = seq // ch
    row = lambda b, h, c: b * nc + c
    return pl.pallas_call(
        functools.partial(_gla_kernel, scale=B_K_DIM ** -0.5, eps=EPS),
        grid=(batch, heads, nc),
        in_specs=[
            pl.BlockSpec((ch, B_K_DIM), lambda b, h, c: (row(b, h, c), cols["bq"] // B_K_DIM + h)),
            pl.BlockSpec((ch, B_K_DIM), lambda b, h, c: (row(b, h, c), cols["bk"] // B_K_DIM + h)),
            pl.BlockSpec((ch, B_V_DIM), lambda b, h, c: (row(b, h, c), cols["bv"] // B_V_DIM + h)),
            pl.BlockSpec((ch, B_V_DIM), lambda b, h, c: (row(b, h, c), cols["og"] // B_V_DIM + h)),
            pl.BlockSpec((ch, LANES), lambda b, h, c: (row(b, h, c), 0)),
            pl.BlockSpec((LANES, B_K_DIM), lambda b, h, c: (0, h)),
            pl.BlockSpec((1, B_K_DIM), lambda b, h, c: (0, h)),
            pl.BlockSpec((1, B_V_DIM), lambda b, h, c: (0, 0)),
        ],
        out_specs=pl.BlockSpec((ch, B_V_DIM), lambda b, h, c: (row(b, h, c), h)),
        out_shape=jax.ShapeDtypeStruct((t, heads * B_V_DIM), BF16),
        scratch_shapes=[pltpu.VMEM((B_V_DIM, B_K_DIM), F32)],
        compiler_params=_params("parallel", "parallel", "arbitrary"),
        name="gla",
    )(p, p, p, p, p_lo, w_gate, b_gate.reshape(1, -1), gn.reshape(1, -1))


def _rwkv_prep_kernel(pd_ref, mu_ref, w0_ref, w2_ref, a0_ref, a2_ref, g2_ref,
                      r_ref, k_ref, v_ref, wl_ref, a_ref, g_ref, carry_sc, *, width, rw, ra):
    @pl.when(pl.program_id(1) == 0)
    def _():
        carry_sc[...] = jnp.zeros(carry_sc.shape, F32)

    x = pd_ref[...]
    n = x.shape[0]
    first = lax.broadcasted_iota(jnp.int32, x.shape, 0) == 0
    prev = jnp.where(first, carry_sc[...], pltpu.roll(x, 1, 0))
    carry_sc[...] = x[n - 1:n, :]
    xs = x + (prev - x) * mu_ref[...]
    r_ref[...] = xs[:, 0:width]
    k_ref[...] = xs[:, width:2 * width]
    v_ref[...] = xs[:, 2 * width:3 * width]
    c0 = 3 * width
    xw, xa, xg = xs[:, c0:c0 + rw], xs[:, c0 + rw:c0 + rw + ra], xs[:, c0 + rw + ra:]
    zw = w0_ref[...] + jnp.dot(jnp.tanh(xw), w2_ref[...], precision=HIGHEST, preferred_element_type=F32)
    w = -(jnp.maximum(-zw, 0.0) + jnp.log1p(jnp.exp(-jnp.abs(zw)))) - 0.5
    wl_ref[...] = -jnp.exp(w)
    za = a0_ref[...] + jnp.dot(xa, a2_ref[...], precision=HIGHEST, preferred_element_type=F32)
    a_ref[...] = jax.nn.sigmoid(za)
    g_ref[...] = jnp.dot(jax.nn.sigmoid(xg), g2_ref[...], precision=HIGHEST, preferred_element_type=F32)


def rwkv_prep(pd, mu, w0, w2, a0, a2, g2, batch, seq):
    t, cols = pd.shape
    width = w0.shape[-1]
    rw, ra, rg = w2.shape[0], a2.shape[0], g2.shape[0]
    tb = _pick(seq, (256, 128, 64, 32, 16, 8))
    nb = seq // tb
    full = lambda shape: pl.BlockSpec(shape, lambda b, s: (0, 0))
    ospec = pl.BlockSpec((tb, width), lambda b, s: (b * nb + s, 0))
    return pl.pallas_call(
        functools.partial(_rwkv_prep_kernel, width=width, rw=rw, ra=ra),
        grid=(batch, nb),
        in_specs=[pl.BlockSpec((tb, cols), lambda b, s: (b * nb + s, 0)), full((1, cols)), full((1, width)),
                  full((rw, width)), full((1, width)), full((ra, width)), full((rg, width))],
        out_specs=[ospec] * 6,
        out_shape=[jax.ShapeDtypeStruct((t, width), F32)] * 6,
        scratch_shapes=[pltpu.VMEM((1, cols), F32)],
        compiler_params=_params("parallel", "arbitrary"),
        name="rwkv_prep",
    )(pd, mu.reshape(1, cols), w0.reshape(1, width), w2, a0.reshape(1, width), a2, g2)


def _rwkv_scan_kernel(r_ref, k_ref, v_ref, wl_ref, a_ref, g_ref, kk_ref, ka_ref, rk_ref, gw_ref, gb_ref,
                      o_ref, st_sc, *, heads, gn_eps):
    @pl.when(pl.program_id(2) == 0)
    def _():
        st_sc[...] = jnp.zeros(st_sc.shape, F32)

    n = r_ref.shape[0]
    incl = _tril(n)
    strict = _tril(n, strict=True)
    eye = (lax.broadcasted_iota(jnp.int32, (n, n), 0) == lax.broadcasted_iota(jnp.int32, (n, n), 1)).astype(F32)
    mm = functools.partial(jnp.dot, precision=HIGHEST, preferred_element_type=F32)
    mm_nt = lambda a, b: lax.dot_general(a, b, NT_DIMS, precision=HIGHEST, preferred_element_type=F32)
    mm_tn = lambda a, b: lax.dot_general(a, b, TN_DIMS, precision=HIGHEST, preferred_element_type=F32)

    cum_all = mm(incl.astype(F32), wl_ref[...])
    outs = []
    for h in range(heads):
        sl = slice(h * D_HEAD, (h + 1) * D_HEAD)
        r, k, v, wl, a_s = r_ref[:, sl], k_ref[:, sl], v_ref[:, sl], wl_ref[:, sl], a_ref[:, sl]
        cum = cum_all[:, sl]
        kk = k * kk_ref[:, sl]
        kk = kk * lax.rsqrt(jnp.maximum(jnp.sum(kk * kk, axis=-1, keepdims=True), 1e-24))
        ke = k * (1.0 + (a_s - 1.0) * ka_ref[:, sl])
        cum_last = cum[n - 1:n, :]
        e_neg = jnp.exp(-cum)
        at = -kk * jnp.exp(cum - wl)
        rt = r * jnp.exp(cum)
        bt = kk * a_s * e_neg
        kt = ke * e_neg
        e_end = jnp.exp(cum_last - cum)
        bl = kk * a_s * e_end
        kl = ke * e_end
        s0 = st_sc[h]
        nab = jnp.where(strict, mm_nt(at, bt), 0.0)
        aak = jnp.where(strict, mm_nt(at, kt), 0.0)
        rb = jnp.where(incl, mm_nt(rt, bt), 0.0)
        rkm = jnp.where(incl, mm_nt(rt, kt), 0.0)
        inv = eye + nab
        pw = nab
        steps = max(1, int(math.ceil(math.log2(n))) - 1)
        for _ in range(steps):
            pw = mm(pw, pw)
            inv = inv + mm(inv, pw)
        u = mm(inv, mm_nt(at, s0) + mm(aak, v))
        y = mm_nt(rt, s0) + mm(rb, u) + mm(rkm, v)
        st_sc[h] = s0 * jnp.exp(cum_last) + mm_tn(u, bl) + mm_tn(v, kl)
        mu = jnp.mean(y, axis=-1, keepdims=True)
        var = jnp.mean(jnp.square(y - mu), axis=-1, keepdims=True)
        yn = (y - mu) * lax.rsqrt(var + gn_eps) * gw_ref[:, sl] + gb_ref[:, sl]
        bonus = jnp.sum(r * ke * rk_ref[:, sl], axis=-1, keepdims=True) * v
        outs.append((yn + bonus) * g_ref[:, sl])
    o_ref[...] = jnp.concatenate(outs, axis=-1).astype(o_ref.dtype)


def rwkv_scan(r, k, v, wl, a, g, k_k, k_a, r_k, gn_w, gn_b, batch, seq):
    t, width = r.shape
    hp = RWKV_HEADS_PER_STEP
    gw = hp * D_HEAD
    ch = RWKV_CHUNK
    nc = seq // ch
    xspec = pl.BlockSpec((ch, gw), lambda b, h, c: (b * nc + c, h))
    pspec = pl.BlockSpec((1, gw), lambda b, h, c: (0, h))
    vec = lambda p: p.reshape(1, width).astype(F32)
    return pl.pallas_call(
        functools.partial(_rwkv_scan_kernel, heads=hp, gn_eps=D_GN_EPS),
        grid=(batch, width // gw, nc),
        in_specs=[xspec] * 6 + [pspec] * 5,
        out_specs=xspec,
        out_shape=jax.ShapeDtypeStruct((t, width), BF16),
        scratch_shapes=[pltpu.VMEM((hp, D_HEAD, D_HEAD), F32)],
        compiler_params=_params("parallel", "parallel", "arbitrary"),
        name="rwkv_scan",
    )(r, k, v, wl, a, g, vec(k_k), vec(k_a), vec(r_k), vec(gn_w), vec(gn_b))


def _extract_top(work, dst_ref, count):
    for r in range(count):
        m = jnp.max(work, axis=0, keepdims=True)
        dst_ref[r:r + 1, :] = m
        work = jnp.where(work >= m, -jnp.inf, work)


def _peer_topk_kernel(q_ref, keys_ref, s1_ref, f1_ref, s2_ref, e2_ref, tau_ref, a1_sc, a2_sc, cand_sc, top_sc):
    kk = P_TOPK
    for h in range(P_HEADS):
        sc = []
        for half in range(2):
            qh = q_ref[:, (2 * h + half) * LANES:(2 * h + half + 1) * LANES]
            sc.append(lax.dot_general(keys_ref[h, half], qh, NT_DIMS, precision=HIGHEST,
                                      preferred_element_type=F32))
        _extract_top(sc[0], a1_sc, kk)
        _extract_top(sc[1], a2_sc, kk)
        a2 = a2_sc[...]
        for i in range(kk):
            cand_sc[i * kk:(i + 1) * kk, :] = a1_sc[i:i + 1, :] + a2
        _extract_top(cand_sc[...], top_sc, kk)
        top = top_sc[...]
        best = top[0:1, :]
        tau = top[kk - 1:kk, :]
        zsum = jnp.sum(jnp.exp(top - best), axis=0, keepdims=True)
        s1_ref[h] = sc[0]
        s2_ref[h] = sc[1]
        f1_ref[h] = jnp.exp(sc[0] - a1_sc[0:1, :]) / zsum
        e2_ref[h] = jnp.exp(sc[1] - a2_sc[0:1, :])
        tau_ref[h] = tau


def peer_topk(q, keys):
    t = q.shape[0]
    tt = _pick(t, (256, 128))
    big = pl.BlockSpec((P_HEADS, P_NKEYS, tt), lambda i: (0, 0, i))
    big_shape = jax.ShapeDtypeStruct((P_HEADS, P_NKEYS, t), F32)
    return pl.pallas_call(
        _peer_topk_kernel,
        grid=(t // tt,),
        in_specs=[pl.BlockSpec((tt, q.shape[1]), lambda i: (i, 0)),
                  pl.BlockSpec(keys.shape, lambda i: (0, 0, 0, 0))],
        out_specs=[big, big, big, big, pl.BlockSpec((P_HEADS, 1, tt), lambda i: (0, 0, i))],
        out_shape=[big_shape] * 4 + [jax.ShapeDtypeStruct((P_HEADS, 1, t), F32)],
        scratch_shapes=[pltpu.VMEM((P_TOPK, tt), F32), pltpu.VMEM((P_TOPK, tt), F32),
                        pltpu.VMEM((P_TOPK * P_TOPK, tt), F32), pltpu.VMEM((P_TOPK, tt), F32)],
        compiler_params=_params("parallel"),
        name="peer_topk",
    )(q, keys)


def _peer_dense_kernel(x_ref, u_ref, v_ref, s1_ref, f1_ref, s2_ref, e2_ref, tau_ref, o_ref, *, rows):
    @pl.when(pl.program_id(1) == 0)
    def _():
        o_ref[...] = jnp.zeros(o_ref.shape, F32)

    act = lax.dot_general(u_ref[...], x_ref[...], NT_DIMS, preferred_element_type=F32)
    act = 0.5 * act * (1.0 + lax.erf(act * (2.0 ** -0.5)))
    parts = []
    for r in range(rows):
        coef = None
        for h in range(P_HEADS):
            sel = (s2_ref[h] + s1_ref[h, r:r + 1, :]) >= tau_ref[h]
            c = jnp.where(sel, e2_ref[h] * f1_ref[h, r:r + 1, :], 0.0)
            coef = c if coef is None else coef + c
        parts.append((coef * act[r * P_NKEYS:(r + 1) * P_NKEYS, :]).astype(BF16))
    w = jnp.concatenate(parts, axis=0)
    o_ref[...] += lax.dot_general(w, v_ref[...], TN_DIMS, preferred_element_type=F32)


def peer_dense(x, u, v, s1, f1, s2, e2, tau):
    t, d = x.shape
    n = u.shape[0]
    tt = _pick(t, (512, 256, 128))
    rows = 4
    tn = rows * P_NKEYS
    s1r = s1.reshape(P_HEADS, P_NKEYS // rows, rows, t)
    f1r = f1.reshape(P_HEADS, P_NKEYS // rows, rows, t)
    rowspec = pl.BlockSpec((P_HEADS, None, rows, tt), lambda i, e: (0, e, 0, i))
    fullspec = pl.BlockSpec((P_HEADS, P_NKEYS, tt), lambda i, e: (0, 0, i))
    return pl.pallas_call(
        functools.partial(_peer_dense_kernel, rows=rows),
        grid=(t // tt, n // tn),
        in_specs=[pl.BlockSpec((tt, d), lambda i, e: (i, 0)),
                  pl.BlockSpec((tn, d), lambda i, e: (e, 0)),
                  pl.BlockSpec((tn, d), lambda i, e: (e, 0)),
                  rowspec, rowspec, fullspec, fullspec,
                  pl.BlockSpec((P_HEADS, 1, tt), lambda i, e: (0, 0, i))],
        out_specs=pl.BlockSpec((tt, d), lambda i, e: (i, 0)),
        out_shape=jax.ShapeDtypeStruct((t, d), F32),
        compiler_params=_params("parallel", "arbitrary"),
        name="peer_dense",
    )(x, u, v, s1r, f1r, s2, e2, tau)


def peer_ffn(hn, w_q, keys, u, v):
    q = matmul([hn], [w_q.astype(BF16)], F32)
    s1, f1, s2, e2, tau = peer_topk(q, keys)
    return peer_dense(hn, u.astype(BF16), v.astype(BF16), s1, f1, s2, e2, tau)


def even_mixer(hn, w_in, w_out, lam_params, subln_gain, lam_init, w_gate, b_gate, gla_norm, batch, seq):
    na = w_out.shape[0] // 2
    a_heads = na // A_V_DIM
    b_heads = na // B_V_DIM
    sizes = (2 * a_heads * A_QK_DIM, 2 * a_heads * A_QK_DIM, na, b_heads * B_K_DIM, b_heads * B_K_DIM, na,
             w_gate.shape[0], na)
    names = ("aq", "ak", "av", "bq", "bk", "bv", "lo", "og")
    start, off = {}, 0
    for nm, sz in zip(names, sizes):
        start[nm] = off
        off += sz
    w_main = jnp.concatenate([w_in[:, :start["lo"]], w_in[:, start["og"]:]], axis=1).astype(BF16)
    cols = dict(start)
    cols["og"] = start["lo"]
    w_lo = _pad_cols(w_in[:, start["lo"]:start["og"]], LANES).astype(BF16)
    p = matmul([hn], [w_main], BF16)
    p_lo = matmul([hn], [w_lo], F32)
    qk = rope(p, rope_tables(seq, A_ROT), seq, width=start["av"], col_block=0, half=A_ROT // 2)
    lp = lam_params.astype(F32)
    lam = jnp.exp(jnp.sum(lp[0] * lp[1])) - jnp.exp(jnp.sum(lp[2] * lp[3])) + lam_init
    o_a = diff_attention(qk, p, lam, subln_gain, lam_init, batch, seq, start["av"])
    w_gate_p = jnp.pad(w_gate.astype(F32), ((0, LANES - w_gate.shape[0]), (0, 0)))
    o_b = gla(p, p_lo, w_gate_p, b_gate.astype(F32), gla_norm.astype(F32), batch, seq, cols)
    w_out = w_out.astype(BF16)
    return matmul([o_a, o_b], [w_out[:na], w_out[na:]], F32)


def odd_mixer(hn, w_in, w_out, q_norm, kv_norm, w_uq, w_ukv, mu, w0, w2, a0, a2, g2,
              k_k, k_a, r_k, gn_w, gn_b, batch, seq):
    q_rank, kv_rank = q_norm.shape[0], kv_norm.shape[0]
    width = w0.shape[0]
    rw, ra, rg = w2.shape[0], a2.shape[0], g2.shape[0]
    c_cols = q_rank + kv_rank + C_ROPE
    heads = w_uq.shape[1] // (C_NOPE + C_ROPE)
    c_pad = -(-(q_rank + kv_rank + LANES) // 256) * 256
    w_c = _pad_cols(w_in[:, :c_cols], c_pad).astype(BF16)
    pad_l = lambda m: -(-m // LANES) * LANES
    d0 = c_cols
    segs, mus, off = [], [], d0
    for sz in (width, width, width, rw, ra, rg):
        segs.append(_pad_cols(w_in[:, off:off + sz], pad_l(sz)))
        mus.append(jnp.pad(mu[off - d0:off - d0 + sz], (0, pad_l(sz) - sz)))
        off += sz
    w_d = jnp.concatenate(segs, axis=1).astype(BF16)
    mu_d = jnp.concatenate(mus).astype(F32)
    pad_r = lambda w: jnp.pad(w.astype(F32), ((0, pad_l(w.shape[0]) - w.shape[0]), (0, 0)))

    p_c = matmul([hn], [w_c], F32)
    p_d = matmul([hn], [w_d], F32)

    cq = add_rmsnorm([p_c], q_norm, BF16, width=q_rank, col_block=0)
    ckv = add_rmsnorm([p_c], kv_norm, BF16, width=kv_rank, col_block=q_rank // kv_rank)
    wq3 = w_uq.reshape(q_rank, heads, C_NOPE + C_ROPE)
    wq_nope = wq3[:, :, :C_NOPE].reshape(q_rank, heads * C_NOPE)
    wq_rope = jnp.pad(wq3[:, :, C_NOPE:], ((0, 0), (0, 0), (0, LANES - C_ROPE))).reshape(q_rank, heads * LANES)
    q_all = matmul([cq], [jnp.concatenate([wq_nope, wq_rope], axis=1).astype(BF16)], BF16)
    wkv3 = w_ukv.reshape(kv_rank, heads, C_NOPE + C_V)
    w_kv = jnp.concatenate([wkv3[:, :, :C_NOPE].reshape(kv_rank, heads * C_NOPE),
                            wkv3[:, :, C_NOPE:].reshape(kv_rank, heads * C_V)], axis=1).astype(BF16)
    kv = matmul([ckv], [w_kv], BF16)
    tables = rope_tables(seq, C_ROPE)
    q_rope = rope(q_all, tables, seq, width=heads * LANES, col_block=1, half=C_ROPE // 2)
    k_rope = rope(p_c, tables, seq, width=LANES, col_block=(q_rank + kv_rank) // LANES, half=C_ROPE // 2)
    o_c = mla_attention(q_all, q_rope, kv, k_rope, batch, seq)

    r, k, v, wl, a, g = rwkv_prep(p_d, mu_d, w0.astype(F32), pad_r(w2), a0.astype(F32), pad_r(a2), pad_r(g2),
                                  batch, seq)
    o_d = rwkv_scan(r, k, v, wl, a, g, k_k, k_a, r_k, gn_w, gn_b, batch, seq)
    nc = o_c.shape[1]
    w_out = w_out.astype(BF16)
    return matmul([o_c, o_d], [w_out[:nc], w_out[nc:]], F32)


def kernel(x, norm_mix, norm_ffn, norm_final, even_w_in, even_w_out, diff_lambda, diff_subln, gla_w_gate, gla_b_gate, gla_norm, odd_w_in, odd_w_out, mla_q_norm, mla_kv_norm, mla_w_uq, mla_w_ukv, rwkv_mu, rwkv_w0, rwkv_w2, rwkv_a0, rwkv_a2, rwkv_g2, rwkv_k_k, rwkv_k_a, rwkv_r_k, rwkv_gn_w, rwkv_gn_b, peer_w_q, peer_keys, peer_u, peer_v):
    batch, seq, d = x.shape
    depth = norm_mix.shape[0]
    h = x.reshape(batch * seq, d)
    pending = []
    for layer in range(depth):
        j = layer // 2
        if pending:
            h, hn = add_rmsnorm([h] + pending, norm_mix[layer], BF16, emit_sum=True)
        else:
            hn = add_rmsnorm([h], norm_mix[layer], BF16)
        if layer % 2 == 0:
            lam_init = 0.8 - 0.6 * math.exp(-0.3 * layer)
            mix = even_mixer(hn, even_w_in[j], even_w_out[j], diff_lambda[j], diff_subln[j], lam_init,
                             gla_w_gate[j], gla_b_gate[j], gla_norm[j], batch, seq)
        else:
            mix = odd_mixer(hn, odd_w_in[j], odd_w_out[j], mla_q_norm[j], mla_kv_norm[j], mla_w_uq[j],
                            mla_w_ukv[j], rwkv_mu[j], rwkv_w0[j], rwkv_w2[j], rwkv_a0[j], rwkv_a2[j],
                            rwkv_g2[j], rwkv_k_k[j], rwkv_k_a[j], rwkv_r_k[j], rwkv_gn_w[j], rwkv_gn_b[j],
                            batch, seq)
        h, hn2 = add_rmsnorm([h, mix], norm_ffn[layer], BF16, emit_sum=True)
        pending = [peer_ffn(hn2, peer_w_q[layer], peer_keys[layer], peer_u[layer], peer_v[layer])]
    out = add_rmsnorm([h] + pending, norm_final, F32)
    return out.reshape(batch, seq, d).astype(x.dtype)
```

```python
import functools
import math

import jax
import jax.numpy as jnp
from jax import lax
from jax.experimental import pallas as pl
from jax.experimental.pallas import tpu as pltpu

F32 = jnp.float32
BF16 = jnp.bfloat16
HIGHEST = lax.Precision.HIGHEST

LANES = 128
VMEM_LIMIT = 56 * 1024 * 1024

EPS = 1e-6
ROPE_THETA = 500000.0
A_QK_DIM = 128
A_V_DIM = 256
A_ROT = A_QK_DIM // 4
B_V_DIM = 512
B_K_DIM = 256
B_GATE_TAU = 16.0
GLA_CHUNK = 64
C_NOPE = 128
C_ROPE = 64
C_V = 128
D_HEAD = 64
D_GN_EPS = 64e-5
RWKV_CHUNK = 64
RWKV_HEADS_PER_STEP = 4
RWKV_GROUPS_PER_STEP = 2
P_HEADS = 8
P_NKEYS = 128
P_TOPK = 16

NT_DIMS = (((1,), (1,)), ((), ()))
TN_DIMS = (((0,), (0,)), ((), ()))


def _params(*sem):
    return pltpu.CompilerParams(dimension_semantics=sem, vmem_limit_bytes=VMEM_LIMIT)


def _pick(n, cands):
    for c in cands:
        if n % c == 0:
            return c
    raise ValueError(f"no tile in {cands} divides {n}")


def _pad_cols(w, n):
    return jnp.pad(w, ((0, 0), (0, n - w.shape[1])))


def _norm_kernel(*refs, eps, n_add, emit_sum):
    xs = refs[:n_add]
    g_ref = refs[n_add]
    outs = refs[n_add + 1:]
    x = xs[0][...].astype(F32)
    for r in xs[1:]:
        x = x + r[...].astype(F32)
    y = x * lax.rsqrt(jnp.mean(x * x, axis=-1, keepdims=True) + eps) * g_ref[...]
    if emit_sum:
        outs[0][...] = x
        outs[1][...] = y.astype(outs[1].dtype)
    else:
        outs[0][...] = y.astype(outs[0].dtype)


def add_rmsnorm(xs, g, out_dtype, *, emit_sum=False, width=None, col_block=0, eps=EPS):
    t = xs[0].shape[0]
    width = width or xs[0].shape[1]
    tr = _pick(t, (128, 64, 32, 16, 8))
    spec = pl.BlockSpec((tr, width), lambda i: (i, col_block))
    ospec = pl.BlockSpec((tr, width), lambda i: (i, 0))
    out_shape = [jax.ShapeDtypeStruct((t, width), out_dtype)]
    out_specs = [ospec]
    if emit_sum:
        out_shape = [jax.ShapeDtypeStruct((t, width), F32)] + out_shape
        out_specs = [ospec, ospec]
    res = pl.pallas_call(
        functools.partial(_norm_kernel, eps=eps, n_add=len(xs), emit_sum=emit_sum),
        grid=(t // tr,),
        in_specs=[spec] * len(xs) + [pl.BlockSpec((1, width), lambda i: (0, 0))],
        out_specs=out_specs,
        out_shape=out_shape,
        compiler_params=_params("parallel"),
        name="add_rmsnorm",
    )(*xs, g.reshape(1, width).astype(F32))
    return res if emit_sum else res[0]


def _mm_kernel(*refs, n_in):
    o_ref = refs[2 * n_in]
    acc = jnp.dot(refs[0][...], refs[n_in][...], preferred_element_type=F32)
    for i in range(1, n_in):
        acc = acc + jnp.dot(refs[i][...], refs[n_in + i][...], preferred_element_type=F32)
    o_ref[...] = acc.astype(o_ref.dtype)


def matmul(xs, ws, out_dtype):
    m = xs[0].shape[0]
    n = ws[0].shape[1]
    tm = _pick(m, (1024, 512, 256, 128, 64, 32, 16))
    tn = _pick(n, (512, 768, 640, 256, 128))
    in_specs = [pl.BlockSpec((tm, x.shape[1]), lambda i, j: (i, 0)) for x in xs]
    in_specs += [pl.BlockSpec((w.shape[0], tn), lambda i, j: (0, j)) for w in ws]
    return pl.pallas_call(
        functools.partial(_mm_kernel, n_in=len(xs)),
        grid=(m // tm, n // tn),
        in_specs=in_specs,
        out_specs=pl.BlockSpec((tm, tn), lambda i, j: (i, j)),
        out_shape=jax.ShapeDtypeStruct((m, n), out_dtype),
        compiler_params=_params("parallel", "arbitrary"),
        name="matmul",
    )(*xs, *ws)


def rope_tables(seq, rot_dim):
    half = rot_dim // 2
    inv = 1.0 / (ROPE_THETA ** (jnp.arange(0, rot_dim, 2, dtype=F32) / rot_dim))
    ang = jnp.arange(seq, dtype=F32)[:, None] * inv[None, :]
    cos, sin = jnp.cos(ang), jnp.sin(ang)
    z = jnp.zeros((seq, LANES - 2 * half), F32)
    zh = jnp.zeros((seq, half), F32)
    c = jnp.concatenate([cos, cos, jnp.ones_like(z)], axis=1)
    s_up = jnp.concatenate([zh, sin, z], axis=1)
    s_dn = jnp.concatenate([-sin, zh, z], axis=1)
    return c, s_up, s_dn


def _rope_kernel(x_ref, c_ref, su_ref, sd_ref, o_ref, *, half, groups):
    c, su, sd = c_ref[...], su_ref[...], sd_ref[...]
    for g in range(groups):
        sl = slice(g * LANES, (g + 1) * LANES)
        x = x_ref[:, sl].astype(F32)
        y = x * c + pltpu.roll(x, half, 1) * su + pltpu.roll(x, LANES - half, 1) * sd
        o_ref[:, sl] = y.astype(o_ref.dtype)


def rope(x, tables, seq, *, width, col_block, half):
    t = x.shape[0]
    tr = _pick(seq, (256, 128, 64, 32, 16))
    nb = seq // tr
    tspec = pl.BlockSpec((tr, LANES), lambda i: (i % nb, 0))
    return pl.pallas_call(
        functools.partial(_rope_kernel, half=half, groups=width // LANES),
        grid=(t // tr,),
        in_specs=[pl.BlockSpec((tr, width), lambda i: (i, col_block)), tspec, tspec, tspec],
        out_specs=pl.BlockSpec((tr, width), lambda i: (i, 0)),
        out_shape=jax.ShapeDtypeStruct((t, width), BF16),
        compiler_params=_params("parallel"),
        name="rope",
    )(x, *tables)


LOG2E = 1.4426950408889634


def _scaled_q(q, scale):
    return (q.astype(F32) * (scale * LOG2E)).astype(BF16)


def _softmax_block(q, k, v, m_ref, l_ref, acc_ref, mask):
    s = lax.dot_general(q, k, NT_DIMS, preferred_element_type=F32)
    if mask is not None:
        s = jnp.where(mask, s, -jnp.inf)
    m_prev = m_ref[...]
    m_new = jnp.maximum(m_prev, jnp.max(s, axis=-1, keepdims=True))
    alpha = jnp.exp2(m_prev - m_new)
    p = jnp.exp2(s - m_new)
    l_ref[...] = alpha * l_ref[...] + jnp.sum(p, axis=-1, keepdims=True)
    acc_ref[...] = alpha * acc_ref[...] + jnp.dot(p.astype(v.dtype), v, preferred_element_type=F32)
    m_ref[...] = m_new


def _causal_sweep(i, tq, step):
    def body(j, carry):
        step(pl.ds(pl.multiple_of(j * tq, tq), tq), None)
        return carry
    lax.fori_loop(0, i, body, 0)
    r = lax.broadcasted_iota(jnp.int32, (tq, tq), 0)
    c = lax.broadcasted_iota(jnp.int32, (tq, tq), 1)
    step(pl.ds(pl.multiple_of(i * tq, tq), tq), c <= r)


def _diff_attn_kernel(lam_ref, q_ref, k_ref, v_ref, g_ref, o_ref, m_sc, l_sc, acc_sc, *, scale, tq, out_scale, eps):
    i = pl.program_id(2)
    m_sc[...] = jnp.full(m_sc.shape, -jnp.inf, F32)
    l_sc[...] = jnp.zeros(l_sc.shape, F32)
    acc_sc[...] = jnp.zeros(acc_sc.shape, F32)
    qs = [_scaled_q(q_ref[:, mp * A_QK_DIM:(mp + 1) * A_QK_DIM], scale) for mp in range(2)]

    def step(rows, mask):
        v = v_ref[rows, :]
        for mp in range(2):
            k = k_ref[rows, mp * A_QK_DIM:(mp + 1) * A_QK_DIM]
            _softmax_block(qs[mp], k, v, m_sc.at[mp], l_sc.at[mp], acc_sc.at[mp], mask)

    _causal_sweep(i, tq, step)
    o = acc_sc[0] / l_sc[0] - lam_ref[0] * (acc_sc[1] / l_sc[1])
    o = o * lax.rsqrt(jnp.mean(o * o, axis=-1, keepdims=True) + eps) * g_ref[...] * out_scale
    o_ref[...] = o.astype(o_ref.dtype)


def diff_attention(qk, p, lam, subln_gain, lam_init, batch, seq, v_col):
    t = qk.shape[0]
    heads = qk.shape[1] // (4 * A_QK_DIM)
    hw = 2 * A_QK_DIM
    tq = _pick(seq, (512, 256, 128))
    nq = seq // tq
    vb = v_col // A_V_DIM
    kern = functools.partial(_diff_attn_kernel, scale=A_QK_DIM ** -0.5, tq=tq, out_scale=1.0 - lam_init, eps=EPS)
    return pl.pallas_call(
        kern,
        grid=(batch, heads, nq),
        in_specs=[
            pl.BlockSpec(memory_space=pltpu.SMEM),
            pl.BlockSpec((tq, hw), lambda b, h, i: (b * nq + i, h)),
            pl.BlockSpec((seq, hw), lambda b, h, i: (b, heads + h)),
            pl.BlockSpec((seq, A_V_DIM), lambda b, h, i: (b, vb + h)),
            pl.BlockSpec((1, A_V_DIM), lambda b, h, i: (0, 0)),
        ],
        out_specs=pl.BlockSpec((tq, A_V_DIM), lambda b, h, i: (b * nq + i, h)),
        out_shape=jax.ShapeDtypeStruct((t, heads * A_V_DIM), BF16),
        scratch_shapes=[pltpu.VMEM((2, tq, 1), F32), pltpu.VMEM((2, tq, 1), F32), pltpu.VMEM((2, tq, A_V_DIM), F32)],
        compiler_params=_params("parallel", "parallel", "arbitrary"),
        name="diff_attention",
    )(lam.reshape(1).astype(F32), qk, qk, p, subln_gain.reshape(1, A_V_DIM).astype(F32))


def _mla_attn_kernel(qn_ref, qr_ref, kn_ref, kr_ref, v_ref, o_ref, m_sc, l_sc, acc_sc, *, scale, tq):
    i = pl.program_id(2)
    m_sc[...] = jnp.full(m_sc.shape, -jnp.inf, F32)
    l_sc[...] = jnp.zeros(l_sc.shape, F32)
    acc_sc[...] = jnp.zeros(acc_sc.shape, F32)
    q = _scaled_q(jnp.concatenate([qn_ref[...], qr_ref[...]], axis=-1), scale)

    def step(rows, mask):
        k = jnp.concatenate([kn_ref[rows, :], kr_ref[rows, :]], axis=-1)
        _softmax_block(q, k, v_ref[rows, :], m_sc, l_sc, acc_sc, mask)

    _causal_sweep(i, tq, step)
    o_ref[...] = (acc_sc[...] / l_sc[...]).astype(o_ref.dtype)


def mla_attention(q_all, q_rope, kv, k_rope, batch, seq):
    t = kv.shape[0]
    heads = kv.shape[1] // (C_NOPE + C_V)
    tq = _pick(seq, (512, 256, 128))
    nq = seq // tq
    qmap = lambda b, h, i: (b * nq + i, h)
    return pl.pallas_call(
        functools.partial(_mla_attn_kernel, scale=(C_NOPE + C_ROPE) ** -0.5, tq=tq),
        grid=(batch, heads, nq),
        in_specs=[
            pl.BlockSpec((tq, C_NOPE), qmap),
            pl.BlockSpec((tq, LANES), qmap),
            pl.BlockSpec((seq, C_NOPE), lambda b, h, i: (b, h)),
            pl.BlockSpec((seq, LANES), lambda b, h, i: (b, 0)),
            pl.BlockSpec((seq, C_V), lambda b, h, i: (b, heads + h)),
        ],
        out_specs=pl.BlockSpec((tq, C_V), qmap),
        out_shape=jax.ShapeDtypeStruct((t, heads * C_V), BF16),
        scratch_shapes=[pltpu.VMEM((tq, 1), F32), pltpu.VMEM((tq, 1), F32), pltpu.VMEM((tq, C_V), F32)],
        compiler_params=_params("parallel", "parallel", "arbitrary"),
        name="mla_attention",
    )(q_all, q_rope, kv, k_rope, kv)


def _tril(n, strict=False):
    r = lax.broadcasted_iota(jnp.int32, (n, n), 0)
    c = lax.broadcasted_iota(jnp.int32, (n, n), 1)
    return (r > c) if strict else (r >= c)


def _gla_kernel(q_ref, k_ref, v_ref, og_ref, lo_ref, wg_ref, bg_ref, gn_ref, o_ref, st_sc, *, scale, eps):
    @pl.when(pl.program_id(2) == 0)
    def _():
        st_sc[...] = jnp.zeros(st_sc.shape, F32)

    n = q_ref.shape[0]
    z = jnp.dot(lo_ref[...], wg_ref[...], precision=HIGHEST, preferred_element_type=F32) + bg_ref[...]
    log_a = (jnp.minimum(z, 0.0) - jnp.log1p(jnp.exp(-jnp.abs(z)))) / B_GATE_TAU
    tri = _tril(n)
    b = jnp.dot(tri.astype(F32), log_a, precision=HIGHEST, preferred_element_type=F32)
    b_last = b[n - 1:n, :]
    q = q_ref[...].astype(F32) * scale
    k = k_ref[...].astype(F32)
    v = v_ref[...]
    qe = (q * jnp.exp(b)).astype(BF16)
    ke = (k * jnp.exp(-b)).astype(BF16)
    kl = (k * jnp.exp(b_last - b)).astype(BF16)
    st = st_sc[...]
    o = lax.dot_general(qe, st.astype(BF16), NT_DIMS, preferred_element_type=F32)
    attn = lax.dot_general(qe, ke, NT_DIMS, preferred_element_type=F32)
    attn = jnp.where(tri, attn, 0.0).astype(BF16)
    o = o + jnp.dot(attn, v, preferred_element_type=F32)
    st_sc[...] = st * jnp.exp(b_last) + lax.dot_general(v, kl, TN_DIMS, preferred_element_type=F32)
    o = o * lax.rsqrt(jnp.mean(o * o, axis=-1, keepdims=True) + eps) * gn_ref[...]
    og = og_ref[...].astype(F32)
    o_ref[...] = (o * (og * jax.nn.sigmoid(og))).astype(o_ref.dtype)


def gla(p, p_lo, w_gate, b_gate, gn, batch, seq, cols):
    t = p.shape[0]
    heads = w_gate.shape[1] // B_K_DIM
    ch = GLA_CHUNK
    nc = seq // ch
    row = lambda b, h, c: b * nc + c
    return pl.pallas_call(
        functools.partial(_gla_kernel, scale=B_K_DIM ** -0.5, eps=EPS),
        grid=(batch, heads, nc),
        in_specs=[
            pl.BlockSpec((ch, B_K_DIM), lambda b, h, c: (row(b, h, c), cols["bq"] // B_K_DIM + h)),
            pl.BlockSpec((ch, B_K_DIM), lambda b, h, c: (row(b, h, c), cols["bk"] // B_K_DIM + h)),
            pl.BlockSpec((ch, B_V_DIM), lambda b, h, c: (row(b, h, c), cols["bv"] // B_V_DIM + h)),
            pl.BlockSpec((ch, B_V_DIM), lambda b, h, c: (row(b, h, c), cols["og"] // B_V_DIM + h)),
            pl.BlockSpec((ch, LANES), lambda b, h, c: (row(b, h, c), 0)),
            pl.BlockSpec((LANES, B_K_DIM), lambda b, h, c: (0, h)),
            pl.BlockSpec((1, B_K_DIM), lambda b, h, c: (0, h)),
            pl.BlockSpec((1, B_V_DIM), lambda b, h, c: (0, 0)),
        ],
        out_specs=pl.BlockSpec((ch, B_V_DIM), lambda b, h, c: (row(b, h, c), h)),
        out_shape=jax.ShapeDtypeStruct((t, heads * B_V_DIM), BF16),
        scratch_shapes=[pltpu.VMEM((B_V_DIM, B_K_DIM), F32)],
        compiler_params=_params("parallel", "parallel", "arbitrary"),
        name="gla",
    )(p, p, p, p, p_lo, w_gate, b_gate.reshape(1, -1), gn.reshape(1, -1))


def _rwkv_prep_kernel(pd_ref, mu_ref, w0_ref, w2_ref, a0_ref, a2_ref, g2_ref,
                      r_ref, k_ref, v_ref, wl_ref, a_ref, g_ref, carry_sc, *, width, rw, ra):
    @pl.when(pl.program_id(1) == 0)
    def _():
        carry_sc[...] = jnp.zeros(carry_sc.shape, F32)

    x = pd_ref[...]
    n = x.shape[0]
    first = lax.broadcasted_iota(jnp.int32, x.shape, 0) == 0
    prev = jnp.where(first, carry_sc[...], pltpu.roll(x, 1, 0))
    carry_sc[...] = x[n - 1:n, :]
    xs = x + (prev - x) * mu_ref[...]
    r_ref[...] = xs[:, 0:width]
    k_ref[...] = xs[:, width:2 * width]
    v_ref[...] = xs[:, 2 * width:3 * width]
    c0 = 3 * width
    xw, xa, xg = xs[:, c0:c0 + rw], xs[:, c0 + rw:c0 + rw + ra], xs[:, c0 + rw + ra:]
    zw = w0_ref[...] + jnp.dot(jnp.tanh(xw), w2_ref[...], precision=HIGHEST, preferred_element_type=F32)
    w = -(jnp.maximum(-zw, 0.0) + jnp.log1p(jnp.exp(-jnp.abs(zw)))) - 0.5
    wl_ref[...] = -jnp.exp(w)
    za = a0_ref[...] + jnp.dot(xa, a2_ref[...], precision=HIGHEST, preferred_element_type=F32)
    a_ref[...] = jax.nn.sigmoid(za)
    g_ref[...] = jnp.dot(jax.nn.sigmoid(xg), g2_ref[...], precision=HIGHEST, preferred_element_type=F32)


def rwkv_prep(pd, mu, w0, w2, a0, a2, g2, batch, seq):
    t, cols = pd.shape
    width = w0.shape[-1]
    rw, ra, rg = w2.shape[0], a2.shape[0], g2.shape[0]
    tb = _pick(seq, (256, 128, 64, 32, 16, 8))
    nb = seq // tb
    full = lambda shape: pl.BlockSpec(shape, lambda b, s: (0, 0))
    ospec = pl.BlockSpec((tb, width), lambda b, s: (b * nb + s, 0))
    return pl.pallas_call(
        functools.partial(_rwkv_prep_kernel, width=width, rw=rw, ra=ra),
        grid=(batch, nb),
        in_specs=[pl.BlockSpec((tb, cols), lambda b, s: (b * nb + s, 0)), full((1, cols)), full((1, width)),
                  full((rw, width)), full((1, width)), full((ra, width)), full((rg, width))],
        out_specs=[ospec] * 6,
        out_shape=[jax.ShapeDtypeStruct((t, width), F32)] * 6,
        scratch_shapes=[pltpu.VMEM((1, cols), F32)],
        compiler_params=_params("parallel", "arbitrary"),
        name="rwkv_prep",
    )(pd, mu.reshape(1, cols), w0.reshape(1, width), w2, a0.reshape(1, width), a2, g2)


def _rwkv_scan_kernel(r_ref, k_ref, v_ref, wl_ref, a_ref, g_ref, kk_ref, ka_ref, rk_ref, gw_ref, gb_ref,
                      o_ref, st_sc, *, heads, chunk, gn_eps):
    @pl.when(pl.program_id(2) == 0)
    def _():
        st_sc[...] = jnp.zeros(st_sc.shape, F32)

    n = chunk
    hw = heads * D_HEAD
    rows = heads * n
    ri = lax.broadcasted_iota(jnp.int32, (rows, hw), 0)
    ci = lax.broadcasted_iota(jnp.int32, (rows, hw), 1)
    same_head = (ri // n) == (ci // D_HEAD)
    rt_i = lax.broadcasted_iota(jnp.int32, (rows, rows), 0)
    ct_i = lax.broadcasted_iota(jnp.int32, (rows, rows), 1)
    strict = (rt_i % n) > (ct_i % n)
    incl = (rt_i % n) >= (ct_i % n)
    eye = (rt_i == ct_i).astype(F32)
    li = lax.broadcasted_iota(jnp.int32, (hw, hw), 0)
    lj = lax.broadcasted_iota(jnp.int32, (hw, hw), 1)
    head_ones = ((li // D_HEAD) == (lj // D_HEAD)).astype(BF16)
    tri_n = _tril(n).astype(F32)

    def to_blk(x):
        return jnp.where(same_head, jnp.concatenate([x] * heads, axis=0), 0.0).astype(BF16)

    def from_blk(y):
        out = y[0:n]
        for h in range(1, heads):
            out = out + y[h * n:(h + 1) * n]
        return out

    def head_sum(x):
        hi = x.astype(BF16)
        lo = (x - hi.astype(F32)).astype(BF16)
        s = jnp.dot(jnp.concatenate([hi, lo], axis=0), head_ones, preferred_element_type=F32)
        return s[:n] + s[n:]

    bdot = functools.partial(jnp.dot, preferred_element_type=F32)
    for c, gi in [(c, gi) for c in range(r_ref.shape[0] // n) for gi in range(r_ref.shape[1] // hw)]:
        sl = slice(c * n, (c + 1) * n)
        ln = slice(gi * hw, (gi + 1) * hw)
        r, k, v, wl, a_s = r_ref[sl, ln], k_ref[sl, ln], v_ref[sl, ln], wl_ref[sl, ln], a_ref[sl, ln]
        cum = jnp.dot(tri_n, wl, precision=HIGHEST, preferred_element_type=F32)
        kk = k * kk_ref[:, ln]
        kk = kk * lax.rsqrt(jnp.maximum(head_sum(kk * kk), 1e-24))
        ke = k * (1.0 + (a_s - 1.0) * ka_ref[:, ln])
        cum_last = cum[n - 1:n, :]
        e_neg = jnp.exp(-cum)
        e_end = jnp.exp(cum_last - cum)
        kka = kk * a_s
        lhs = jnp.concatenate([to_blk(-kk * jnp.exp(cum - wl)), to_blk(r * jnp.exp(cum))], axis=0)
        rhs = jnp.concatenate([to_blk(kka * e_neg), to_blk(ke * e_neg)], axis=0)
        ends = jnp.concatenate([to_blk(kka * e_end), to_blk(ke * e_end)], axis=0)
        vb = to_blk(v)
        cross = lax.dot_general(lhs, rhs, NT_DIMS, preferred_element_type=F32)
        nab = jnp.where(strict, cross[:rows, :rows], 0.0)
        aak = jnp.where(strict, cross[:rows, rows:], 0.0).astype(BF16)
        rbk = jnp.concatenate([jnp.where(incl, cross[rows:, :rows], 0.0),
                               jnp.where(incl, cross[rows:, rows:], 0.0)], axis=1).astype(BF16)
        s0 = st_sc[gi]
        ls = lax.dot_general(lhs, s0.astype(BF16), NT_DIMS, preferred_element_type=F32)
        inv = eye + nab
        pw = nab.astype(BF16)
        for _ in range(max(1, int(math.ceil(math.log2(n))) - 1)):
            pw = bdot(pw, pw).astype(BF16)
            inv = inv + bdot(inv.astype(BF16), pw)
        u = bdot(inv.astype(BF16), (ls[:rows] + bdot(aak, vb)).astype(BF16))
        uv = jnp.concatenate([u.astype(BF16), vb], axis=0)
        y = from_blk(ls[rows:] + bdot(rbk, uv))
        st_sc[gi] = s0 * jnp.exp(cum_last) + lax.dot_general(uv, ends, TN_DIMS, preferred_element_type=F32)
        mu = head_sum(y) * (1.0 / D_HEAD)
        yc = y - mu
        var = head_sum(yc * yc) * (1.0 / D_HEAD)
        yn = yc * lax.rsqrt(var + gn_eps) * gw_ref[:, ln] + gb_ref[:, ln]
        bonus = head_sum(r * ke * rk_ref[:, ln]) * v
        o_ref[sl, ln] = ((yn + bonus) * g_ref[sl, ln]).astype(o_ref.dtype)


def rwkv_scan(r, k, v, wl, a, g, k_k, k_a, r_k, gn_w, gn_b, batch, seq):
    t, width = r.shape
    hp = RWKV_HEADS_PER_STEP
    hw = hp * D_HEAD
    gw = _pick(width, (RWKV_GROUPS_PER_STEP * hw, hw))
    ch = RWKV_CHUNK
    tb = _pick(seq, (2 * ch, ch))
    nc = seq // tb
    xspec = pl.BlockSpec((tb, gw), lambda b, h, c: (b * nc + c, h))
    pspec = pl.BlockSpec((1, gw), lambda b, h, c: (0, h))
    vec = lambda p: p.reshape(1, width).astype(F32)
    return pl.pallas_call(
        functools.partial(_rwkv_scan_kernel, heads=hp, chunk=ch, gn_eps=D_GN_EPS),
        grid=(batch, width // gw, nc),
        in_specs=[xspec] * 6 + [pspec] * 5,
        out_specs=xspec,
        out_shape=jax.ShapeDtypeStruct((t, width), BF16),
        scratch_shapes=[pltpu.VMEM((gw // hw, hw, hw), F32)],
        compiler_params=_params("parallel", "parallel", "arbitrary"),
        name="rwkv_scan",
    )(r, k, v, wl, a, g, vec(k_k), vec(k_a), vec(r_k), vec(gn_w), vec(gn_b))


def _extract_top(work, dst_ref, count):
    for r in range(count):
        m = jnp.max(work, axis=0, keepdims=True)
        dst_ref[r:r + 1, :] = m
        work = jnp.where(work >= m, -jnp.inf, work)


def _peer_topk_kernel(q_ref, keys_ref, thr_ref, f1_ref, s2_ref, e2_ref, a1_sc, a2_sc, cand_sc, top_sc):
    kk = P_TOPK
    for h in range(P_HEADS):
        sc = []
        for half in range(2):
            qh = q_ref[:, (2 * h + half) * LANES:(2 * h + half + 1) * LANES]
            sc.append(lax.dot_general(keys_ref[h, half], qh, NT_DIMS, precision=HIGHEST,
                                      preferred_element_type=F32))
        _extract_top(sc[0], a1_sc, kk)
        _extract_top(sc[1], a2_sc, kk)
        a2 = a2_sc[...]
        for i in range(kk):
            cand_sc[i * kk:(i + 1) * kk, :] = a1_sc[i:i + 1, :] + a2
        _extract_top(cand_sc[...], top_sc, kk)
        top = top_sc[...]
        best = top[0:1, :]
        tau = top[kk - 1:kk, :]
        zsum = jnp.sum(jnp.exp(top - best), axis=0, keepdims=True)
        thr = jnp.full(sc[0].shape, jnp.inf, F32)
        for j in range(kk):
            bj = a2_sc[j:j + 1, :]
            thr = jnp.where(sc[0] + bj >= tau, bj, thr)
        thr_ref[h] = thr
        s2_ref[h] = sc[1]
        f1_ref[h] = jnp.exp(sc[0] - a1_sc[0:1, :]) / zsum
        e2_ref[h] = jnp.exp(sc[1] - a2_sc[0:1, :])


def peer_topk(q, keys):
    t = q.shape[0]
    tt = _pick(t, (256, 128))
    big = pl.BlockSpec((P_HEADS, P_NKEYS, tt), lambda i: (0, 0, i))
    big_shape = jax.ShapeDtypeStruct((P_HEADS, P_NKEYS, t), F32)
    return pl.pallas_call(
        _peer_topk_kernel,
        grid=(t // tt,),
        in_specs=[pl.BlockSpec((tt, q.shape[1]), lambda i: (i, 0)),
                  pl.BlockSpec(keys.shape, lambda i: (0, 0, 0, 0))],
        out_specs=[big, big, big, big],
        out_shape=[big_shape] * 4,
        scratch_shapes=[pltpu.VMEM((P_TOPK, tt), F32), pltpu.VMEM((P_TOPK, tt), F32),
                        pltpu.VMEM((P_TOPK * P_TOPK, tt), F32), pltpu.VMEM((P_TOPK, tt), F32)],
        compiler_params=_params("parallel"),
        name="peer_topk",
    )(q, keys)


def _peer_dense_kernel(x_ref, u_ref, v_ref, thr_ref, f1_ref, s2_ref, e2_ref, o_ref, *, rows, chains):
    @pl.when(pl.program_id(1) == 0)
    def _():
        o_ref[...] = jnp.zeros(o_ref.shape, F32)

    tc = x_ref.shape[0] // chains
    hr = rows // 2
    dh = o_ref.shape[1] // 2
    tok = lambda ch: slice(ch * tc, (ch + 1) * tc)

    def activation(ch, half):
        es = slice(half * hr * P_NKEYS, (half + 1) * hr * P_NKEYS)
        return lax.dot_general(u_ref[es, :], x_ref[tok(ch), :], NT_DIMS, preferred_element_type=F32)

    def gated(ch, half, act):
        ts = tok(ch)
        act = 0.5 * act * (1.0 + lax.erf(act * (2.0 ** -0.5)))
        parts = []
        for r in range(hr):
            row = half * hr + r
            coef = None
            for h in range(P_HEADS):
                sel = s2_ref[h, :, ts] >= thr_ref[h, row:row + 1, ts]
                c = jnp.where(sel, e2_ref[h, :, ts] * f1_ref[h, row:row + 1, ts], 0.0)
                coef = c if coef is None else coef + c
            parts.append((coef * act[r * P_NKEYS:(r + 1) * P_NKEYS, :]).astype(BF16))
        return parts

    def project(ch, half, w):
        cs = slice(half * dh, (half + 1) * dh)
        o_ref[tok(ch), cs] += lax.dot_general(w, v_ref[:, cs], TN_DIMS, preferred_element_type=F32)

    acts = [activation(0, 0), activation(0, 1)]
    w_prev = None
    for ch in range(chains):
        nxt, parts = [], []
        for half in range(2):
            parts += gated(ch, half, acts[half])
            if ch + 1 < chains:
                nxt.append(activation(ch + 1, half))
            if w_prev is not None:
                project(ch - 1, half, w_prev)
        w_prev = jnp.concatenate(parts, axis=0)
        acts = nxt
    project(chains - 1, 0, w_prev)
    project(chains - 1, 1, w_prev)


def peer_dense(x, u, v, thr, f1, s2, e2):
    t, d = x.shape
    n = u.shape[0]
    tt = _pick(t, (1024, 512, 256, 128))
    chains = 2 if tt % 256 == 0 else 1
    rows = 4
    tn = rows * P_NKEYS
    thr_r = thr.reshape(P_HEADS, P_NKEYS // rows, rows, t)
    f1r = f1.reshape(P_HEADS, P_NKEYS // rows, rows, t)
    once = pl.Buffered(1)
    rowspec = pl.BlockSpec((P_HEADS, None, rows, tt), lambda i, e: (0, e, 0, i))
    fullspec = pl.BlockSpec((P_HEADS, P_NKEYS, tt), lambda i, e: (0, 0, i), pipeline_mode=once)
    return pl.pallas_call(
        functools.partial(_peer_dense_kernel, rows=rows, chains=chains),
        grid=(t // tt, n // tn),
        in_specs=[pl.BlockSpec((tt, d), lambda i, e: (i, 0), pipeline_mode=once),
                  pl.BlockSpec((tn, d), lambda i, e: (e, 0)),
                  pl.BlockSpec((tn, d), lambda i, e: (e, 0)),
                  rowspec, rowspec, fullspec, fullspec],
        out_specs=pl.BlockSpec((tt, d), lambda i, e: (i, 0), pipeline_mode=once),
        out_shape=jax.ShapeDtypeStruct((t, d), F32),
        compiler_params=_params("parallel", "arbitrary"),
        name="peer_dense",
    )(x, u, v, thr_r, f1r, s2, e2)


def peer_ffn(hn, w_q, keys, u, v):
    q = matmul([hn], [w_q.astype(BF16)], F32)
    thr, f1, s2, e2 = peer_topk(q, keys)
    return peer_dense(hn, u.astype(BF16), v.astype(BF16), thr, f1, s2, e2)


def even_mixer(hn, w_in, w_out, lam_params, subln_gain, lam_init, w_gate, b_gate, gla_norm, batch, seq):
    na = w_out.shape[0] // 2
    a_heads = na // A_V_DIM
    b_heads = na // B_V_DIM
    sizes = (2 * a_heads * A_QK_DIM, 2 * a_heads * A_QK_DIM, na, b_heads * B_K_DIM, b_heads * B_K_DIM, na,
             w_gate.shape[0], na)
    names = ("aq", "ak", "av", "bq", "bk", "bv", "lo", "og")
    start, off = {}, 0
    for nm, sz in zip(names, sizes):
        start[nm] = off
        off += sz
    w_main = jnp.concatenate([w_in[:, :start["lo"]], w_in[:, start["og"]:]], axis=1).astype(BF16)
    cols = dict(start)
    cols["og"] = start["lo"]
    w_lo = _pad_cols(w_in[:, start["lo"]:start["og"]], LANES).astype(BF16)
    p = matmul([hn], [w_main], BF16)
    p_lo = matmul([hn], [w_lo], F32)
    qk = rope(p, rope_tables(seq, A_ROT), seq, width=start["av"], col_block=0, half=A_ROT // 2)
    lp = lam_params.astype(F32)
    lam = jnp.exp(jnp.sum(lp[0] * lp[1])) - jnp.exp(jnp.sum(lp[2] * lp[3])) + lam_init
    o_a = diff_attention(qk, p, lam, subln_gain, lam_init, batch, seq, start["av"])
    w_gate_p = jnp.pad(w_gate.astype(F32), ((0, LANES - w_gate.shape[0]), (0, 0)))
    o_b = gla(p, p_lo, w_gate_p, b_gate.astype(F32), gla_norm.astype(F32), batch, seq, cols)
    w_out = w_out.astype(BF16)
    return matmul([o_a, o_b], [w_out[:na], w_out[na:]], F32)


def odd_mixer(hn, w_in, w_out, q_norm, kv_norm, w_uq, w_ukv, mu, w0, w2, a0, a2, g2,
              k_k, k_a, r_k, gn_w, gn_b, batch, seq):
    q_rank, kv_rank = q_norm.shape[0], kv_norm.shape[0]
    width = w0.shape[0]
    rw, ra, rg = w2.shape[0], a2.shape[0], g2.shape[0]
    c_cols = q_rank + kv_rank + C_ROPE
    heads = w_uq.shape[1] // (C_NOPE + C_ROPE)
    c_pad = -(-(q_rank + kv_rank + LANES) // 256) * 256
    w_c = _pad_cols(w_in[:, :c_cols], c_pad).astype(BF16)
    pad_l = lambda m: -(-m // LANES) * LANES
    d0 = c_cols
    segs, mus, off = [], [], d0
    for sz in (width, width, width, rw, ra, rg):
        segs.append(_pad_cols(w_in[:, off:off + sz], pad_l(sz)))
        mus.append(jnp.pad(mu[off - d0:off - d0 + sz], (0, pad_l(sz) - sz)))
        off += sz
    w_d = jnp.concatenate(segs, axis=1).astype(BF16)
    mu_d = jnp.concatenate(mus).astype(F32)
    pad_r = lambda w: jnp.pad(w.astype(F32), ((0, pad_l(w.shape[0]) - w.shape[0]), (0, 0)))

    p_c = matmul([hn], [w_c], F32)
    p_d = matmul([hn], [w_d], F32)

    cq = add_rmsnorm([p_c], q_norm, BF16, width=q_rank, col_block=0)
    ckv = add_rmsnorm([p_c], kv_norm, BF16, width=kv_rank, col_block=q_rank // kv_rank)
    wq3 = w_uq.reshape(q_rank, heads, C_NOPE + C_ROPE)
    wq_nope = wq3[:, :, :C_NOPE].reshape(q_rank, heads * C_NOPE)
    wq_rope = jnp.pad(wq3[:, :, C_NOPE:], ((0, 0), (0, 0), (0, LANES - C_ROPE))).reshape(q_rank, heads * LANES)
    q_all = matmul([cq], [jnp.concatenate([wq_nope, wq_rope], axis=1).astype(BF16)], BF16)
    wkv3 = w_ukv.reshape(kv_rank, heads, C_NOPE + C_V)
    w_kv = jnp.concatenate([wkv3[:, :, :C_NOPE].reshape(kv_rank, heads * C_NOPE),
                            wkv3[:, :, C_NOPE:].reshape(kv_rank, heads * C_V)], axis=1).astype(BF16)
    kv = matmul([ckv], [w_kv], BF16)
    tables = rope_tables(seq, C_ROPE)
    q_rope = rope(q_all, tables, seq, width=heads * LANES, col_block=1, half=C_ROPE // 2)
    k_rope = rope(p_c, tables, seq, width=LANES, col_block=(q_rank + kv_rank) // LANES, half=C_ROPE // 2)
    o_c = mla_attention(q_all, q_rope, kv, k_rope, batch, seq)

    r, k, v, wl, a, g = rwkv_prep(p_d, mu_d, w0.astype(F32), pad_r(w2), a0.astype(F32), pad_r(a2), pad_r(g2),
                                  batch, seq)
    o_d = rwkv_scan(r, k, v, wl, a, g, k_k, k_a, r_k, gn_w, gn_b, batch, seq)
    nc = o_c.shape[1]
    w_out = w_out.astype(BF16)
    return matmul([o_c, o_d], [w_out[:nc], w_out[nc:]], F32)


def kernel(x, norm_mix, norm_ffn, norm_final, even_w_in, even_w_out, diff_lambda, diff_subln, gla_w_gate, gla_b_gate, gla_norm, odd_w_in, odd_w_out, mla_q_norm, mla_kv_norm, mla_w_uq, mla_w_ukv, rwkv_mu, rwkv_w0, rwkv_w2, rwkv_a0, rwkv_a2, rwkv_g2, rwkv_k_k, rwkv_k_a, rwkv_r_k, rwkv_gn_w, rwkv_gn_b, peer_w_q, peer_keys, peer_u, peer_v):
    batch, seq, d = x.shape
    depth = norm_mix.shape[0]
    h = x.reshape(batch * seq, d)
    pending = []
    for layer in range(depth):
        j = layer // 2
        if pending:
            h, hn = add_rmsnorm([h] + pending, norm_mix[layer], BF16, emit_sum=True)
        else:
            hn = add_rmsnorm([h], norm_mix[layer], BF16)
        if layer % 2 == 0:
            lam_init = 0.8 - 0.6 * math.exp(-0.3 * layer)
            mix = even_mixer(hn, even_w_in[j], even_w_out[j], diff_lambda[j], diff_subln[j], lam_init,
                             gla_w_gate[j], gla_b_gate[j], gla_norm[j], batch, seq)
        else:
            mix = odd_mixer(hn, odd_w_in[j], odd_w_out[j], mla_q_norm[j], mla_kv_norm[j], mla_w_uq[j],
                            mla_w_ukv[j], rwkv_mu[j], rwkv_w0[j], rwkv_w2[j], rwkv_a0[j], rwkv_a2[j],
                            rwkv_g2[j], rwkv_k_k[j], rwkv_k_a[j], rwkv_r_k[j], rwkv_gn_w[j], rwkv_gn_b[j],
                            batch, seq)
        h, hn2 = add_rmsnorm([h, mix], norm_ffn[layer], BF16, emit_sum=True)
        pending = [peer_ffn(hn2, peer_w_q[layer], peer_keys[layer], peer_u[layer], peer_v[layer])]
    out = add_rmsnorm([h] + pending, norm_final, F32)
    return out.reshape(batch, seq, d).astype(x.dtype)
```

```python
import functools
import math

import jax
import jax.numpy as jnp
from jax import lax
from jax.experimental import pallas as pl
from jax.experimental.pallas import tpu as pltpu

F32 = jnp.float32
BF16 = jnp.bfloat16
HIGHEST = lax.Precision.HIGHEST

LANES = 128
VMEM_LIMIT = 56 * 1024 * 1024

EPS = 1e-6
ROPE_THETA = 500000.0
A_QK_DIM = 128
A_V_DIM = 256
A_ROT = A_QK_DIM // 4
B_V_DIM = 512
B_K_DIM = 256
B_GATE_TAU = 16.0
GLA_CHUNK = 64
C_NOPE = 128
C_ROPE = 64
C_V = 128
D_HEAD = 64
D_GN_EPS = 64e-5
RWKV_CHUNK = 64
ATTN_BLOCKS = (1024, 512, 256, 128)
ATTN_SUB = 4
RWKV_HEADS_PER_STEP = 4
RWKV_GROUPS_PER_STEP = 2
P_HEADS = 8
P_NKEYS = 128
P_TOPK = 16

NT_DIMS = (((1,), (1,)), ((), ()))
TN_DIMS = (((0,), (0,)), ((), ()))


def _params(*sem):
    return pltpu.CompilerParams(dimension_semantics=sem, vmem_limit_bytes=VMEM_LIMIT)


def _pick(n, cands):
    for c in cands:
        if n % c == 0:
            return c
    raise ValueError(f"no tile in {cands} divides {n}")


def _pad_cols(w, n):
    return jnp.pad(w, ((0, 0), (0, n - w.shape[1])))


def _norm_kernel(*refs, eps, n_add, emit_sum):
    xs = refs[:n_add]
    g_ref = refs[n_add]
    outs = refs[n_add + 1:]
    x = xs[0][...].astype(F32)
    for r in xs[1:]:
        x = x + r[...].astype(F32)
    y = x * lax.rsqrt(jnp.mean(x * x, axis=-1, keepdims=True) + eps) * g_ref[...]
    if emit_sum:
        outs[0][...] = x
        outs[1][...] = y.astype(outs[1].dtype)
    else:
        outs[0][...] = y.astype(outs[0].dtype)


def add_rmsnorm(xs, g, out_dtype, *, emit_sum=False, width=None, col_block=0, eps=EPS):
    t = xs[0].shape[0]
    width = width or xs[0].shape[1]
    tr = _pick(t, (128, 64, 32, 16, 8))
    spec = pl.BlockSpec((tr, width), lambda i: (i, col_block))
    ospec = pl.BlockSpec((tr, width), lambda i: (i, 0))
    out_shape = [jax.ShapeDtypeStruct((t, width), out_dtype)]
    out_specs = [ospec]
    if emit_sum:
        out_shape = [jax.ShapeDtypeStruct((t, width), F32)] + out_shape
        out_specs = [ospec, ospec]
    res = pl.pallas_call(
        functools.partial(_norm_kernel, eps=eps, n_add=len(xs), emit_sum=emit_sum),
        grid=(t // tr,),
        in_specs=[spec] * len(xs) + [pl.BlockSpec((1, width), lambda i: (0, 0))],
        out_specs=out_specs,
        out_shape=out_shape,
        compiler_params=_params("parallel"),
        name="add_rmsnorm",
    )(*xs, g.reshape(1, width).astype(F32))
    return res if emit_sum else res[0]


def _mm_kernel(*refs, n_in):
    o_ref = refs[2 * n_in]
    acc = jnp.dot(refs[0][...], refs[n_in][...], preferred_element_type=F32)
    for i in range(1, n_in):
        acc = acc + jnp.dot(refs[i][...], refs[n_in + i][...], preferred_element_type=F32)
    o_ref[...] = acc.astype(o_ref.dtype)


def matmul(xs, ws, out_dtype):
    m = xs[0].shape[0]
    n = ws[0].shape[1]
    tm = _pick(m, (1024, 512, 256, 128, 64, 32, 16))
    tn = _pick(n, (512, 768, 640, 256, 128))
    in_specs = [pl.BlockSpec((tm, x.shape[1]), lambda i, j: (i, 0)) for x in xs]
    in_specs += [pl.BlockSpec((w.shape[0], tn), lambda i, j: (0, j)) for w in ws]
    return pl.pallas_call(
        functools.partial(_mm_kernel, n_in=len(xs)),
        grid=(m // tm, n // tn),
        in_specs=in_specs,
        out_specs=pl.BlockSpec((tm, tn), lambda i, j: (i, j)),
        out_shape=jax.ShapeDtypeStruct((m, n), out_dtype),
        compiler_params=_params("parallel", "arbitrary"),
        name="matmul",
    )(*xs, *ws)


def rope_tables(seq, rot_dim):
    half = rot_dim // 2
    inv = 1.0 / (ROPE_THETA ** (jnp.arange(0, rot_dim, 2, dtype=F32) / rot_dim))
    ang = jnp.arange(seq, dtype=F32)[:, None] * inv[None, :]
    cos, sin = jnp.cos(ang), jnp.sin(ang)
    z = jnp.zeros((seq, LANES - 2 * half), F32)
    zh = jnp.zeros((seq, half), F32)
    c = jnp.concatenate([cos, cos, jnp.ones_like(z)], axis=1)
    s_up = jnp.concatenate([zh, sin, z], axis=1)
    s_dn = jnp.concatenate([-sin, zh, z], axis=1)
    return c, s_up, s_dn


def _rope_kernel(x_ref, c_ref, su_ref, sd_ref, o_ref, *, half, groups):
    c, su, sd = c_ref[...], su_ref[...], sd_ref[...]
    for g in range(groups):
        sl = slice(g * LANES, (g + 1) * LANES)
        x = x_ref[:, sl].astype(F32)
        y = x * c + pltpu.roll(x, half, 1) * su + pltpu.roll(x, LANES - half, 1) * sd
        o_ref[:, sl] = y.astype(o_ref.dtype)


def rope(x, tables, seq, *, width, col_block, half):
    t = x.shape[0]
    tr = _pick(seq, (256, 128, 64, 32, 16))
    nb = seq // tr
    tspec = pl.BlockSpec((tr, LANES), lambda i: (i % nb, 0))
    return pl.pallas_call(
        functools.partial(_rope_kernel, half=half, groups=width // LANES),
        grid=(t // tr,),
        in_specs=[pl.BlockSpec((tr, width), lambda i: (i, col_block)), tspec, tspec, tspec],
        out_specs=pl.BlockSpec((tr, width), lambda i: (i, 0)),
        out_shape=jax.ShapeDtypeStruct((t, width), BF16),
        compiler_params=_params("parallel"),
        name="rope",
    )(x, *tables)


LOG2E = 1.4426950408889634


def _scaled_q(q, scale):
    return (q.astype(F32) * (scale * LOG2E)).astype(BF16)


def _softmax_chains(chains):
    s = [lax.dot_general(q, k, NT_DIMS, preferred_element_type=F32) for q, k, *_ in chains]
    def masked(x, tail):
        if tail is None:
            return x
        w = tail.shape[1]
        last = jnp.where(tail, x[:, x.shape[1] - w:], -jnp.inf)
        return last if w == x.shape[1] else jnp.concatenate([x[:, :x.shape[1] - w], last], axis=1)

    s = [masked(x, c[6]) for x, c in zip(s, chains)]
    m_prev = [c[3][...] for c in chains]
    m_new = [jnp.maximum(mp, jnp.max(x, axis=-1, keepdims=True)) for mp, x in zip(m_prev, s)]
    alpha = [jnp.exp2(mp - mn) for mp, mn in zip(m_prev, m_new)]
    p = [jnp.exp2(x - mn) for x, mn in zip(s, m_new)]
    pv = [jnp.dot(x.astype(c[2].dtype), c[2], preferred_element_type=F32) for x, c in zip(p, chains)]
    for c, a, x, y, mn in zip(chains, alpha, p, pv, m_new):
        c[4][...] = a * c[4][...] + jnp.sum(x, axis=-1, keepdims=True)
        c[5][...] = a * c[5][...] + y
        c[3][...] = mn


def _causal_sweep(i, tq, step):
    def body(j, carry):
        step(pl.ds(pl.multiple_of(j * tq, tq), tq), False)
        return carry
    lax.fori_loop(0, i, body, 0)
    step(pl.ds(pl.multiple_of(i * tq, tq), tq), True)


def _row_groups(q, k, v, m_sc, l_sc, acc_sc, diagonal):
    rs = q.shape[0] // ATTN_SUB
    chains = []
    tri = _tril(rs)
    for g in range(ATTN_SUB):
        qg = q[g * rs:(g + 1) * rs]
        if diagonal:
            n = (g + 1) * rs
            chains.append((qg, k[:n], v[:n], m_sc.at[g], l_sc.at[g], acc_sc.at[g], tri))
        else:
            chains.append((qg, k, v, m_sc.at[g], l_sc.at[g], acc_sc.at[g], None))
    return chains


def _diff_attn_kernel(lam_ref, q_ref, k_ref, v_ref, g_ref, o_ref, m_sc, l_sc, acc_sc, *, scale, tq, out_scale, eps):
    i = pl.program_id(2)
    m_sc[...] = jnp.full(m_sc.shape, -jnp.inf, F32)
    l_sc[...] = jnp.zeros(l_sc.shape, F32)
    acc_sc[...] = jnp.zeros(acc_sc.shape, F32)
    qs = [_scaled_q(q_ref[:, mp * A_QK_DIM:(mp + 1) * A_QK_DIM], scale) for mp in range(2)]

    def step(rows, diagonal):
        v = v_ref[rows, :]
        chains = []
        for mp in range(2):
            k = k_ref[rows, mp * A_QK_DIM:(mp + 1) * A_QK_DIM]
            chains += _row_groups(qs[mp], k, v, m_sc.at[mp], l_sc.at[mp], acc_sc.at[mp], diagonal)
        _softmax_chains(chains)

    _causal_sweep(i, tq, step)
    rs = tq // ATTN_SUB
    for g in range(ATTN_SUB):
        o = acc_sc[0, g] / l_sc[0, g] - lam_ref[0] * (acc_sc[1, g] / l_sc[1, g])
        o = o * lax.rsqrt(jnp.mean(o * o, axis=-1, keepdims=True) + eps) * g_ref[...] * out_scale
        o_ref[g * rs:(g + 1) * rs, :] = o.astype(o_ref.dtype)


def diff_attention(qk, p, lam, subln_gain, lam_init, batch, seq, v_col):
    t = qk.shape[0]
    heads = qk.shape[1] // (4 * A_QK_DIM)
    hw = 2 * A_QK_DIM
    tq = _pick(seq, ATTN_BLOCKS)
    rs = tq // ATTN_SUB
    nq = seq // tq
    vb = v_col // A_V_DIM
    kern = functools.partial(_diff_attn_kernel, scale=A_QK_DIM ** -0.5, tq=tq, out_scale=1.0 - lam_init, eps=EPS)
    return pl.pallas_call(
        kern,
        grid=(batch, heads, nq),
        in_specs=[
            pl.BlockSpec(memory_space=pltpu.SMEM),
            pl.BlockSpec((tq, hw), lambda b, h, i: (b * nq + i, h)),
            pl.BlockSpec((seq, hw), lambda b, h, i: (b, heads + h)),
            pl.BlockSpec((seq, A_V_DIM), lambda b, h, i: (b, vb + h)),
            pl.BlockSpec((1, A_V_DIM), lambda b, h, i: (0, 0)),
        ],
        out_specs=pl.BlockSpec((tq, A_V_DIM), lambda b, h, i: (b * nq + i, h)),
        out_shape=jax.ShapeDtypeStruct((t, heads * A_V_DIM), BF16),
        scratch_shapes=[pltpu.VMEM((2, ATTN_SUB, rs, 1), F32), pltpu.VMEM((2, ATTN_SUB, rs, 1), F32),
                        pltpu.VMEM((2, ATTN_SUB, rs, A_V_DIM), F32)],
        compiler_params=_params("parallel", "parallel", "arbitrary"),
        name="diff_attention",
    )(lam.reshape(1).astype(F32), qk, qk, p, subln_gain.reshape(1, A_V_DIM).astype(F32))


def _mla_attn_kernel(qn_ref, qr_ref, kn_ref, kr_ref, v_ref, o_ref, m_sc, l_sc, acc_sc, *, scale, tq):
    i = pl.program_id(2)
    m_sc[...] = jnp.full(m_sc.shape, -jnp.inf, F32)
    l_sc[...] = jnp.zeros(l_sc.shape, F32)
    acc_sc[...] = jnp.zeros(acc_sc.shape, F32)
    q = _scaled_q(jnp.concatenate([qn_ref[...], qr_ref[...]], axis=-1), scale)

    def step(rows, diagonal):
        k = jnp.concatenate([kn_ref[rows, :], kr_ref[rows, :]], axis=-1)
        _softmax_chains(_row_groups(q, k, v_ref[rows, :], m_sc, l_sc, acc_sc, diagonal))

    _causal_sweep(i, tq, step)
    rs = tq // ATTN_SUB
    for g in range(ATTN_SUB):
        o_ref[g * rs:(g + 1) * rs, :] = (acc_sc[g] / l_sc[g]).astype(o_ref.dtype)


def mla_attention(q_all, q_rope, kv, k_rope, batch, seq):
    t = kv.shape[0]
    heads = kv.shape[1] // (C_NOPE + C_V)
    tq = _pick(seq, ATTN_BLOCKS)
    rs = tq // ATTN_SUB
    nq = seq // tq
    qmap = lambda b, h, i: (b * nq + i, h)
    return pl.pallas_call(
        functools.partial(_mla_attn_kernel, scale=(C_NOPE + C_ROPE) ** -0.5, tq=tq),
        grid=(batch, heads, nq),
        in_specs=[
            pl.BlockSpec((tq, C_NOPE), qmap),
            pl.BlockSpec((tq, LANES), qmap),
            pl.BlockSpec((seq, C_NOPE), lambda b, h, i: (b, h)),
            pl.BlockSpec((seq, LANES), lambda b, h, i: (b, 0)),
            pl.BlockSpec((seq, C_V), lambda b, h, i: (b, heads + h)),
        ],
        out_specs=pl.BlockSpec((tq, C_V), qmap),
        out_shape=jax.ShapeDtypeStruct((t, heads * C_V), BF16),
        scratch_shapes=[pltpu.VMEM((ATTN_SUB, rs, 1), F32), pltpu.VMEM((ATTN_SUB, rs, 1), F32),
                        pltpu.VMEM((ATTN_SUB, rs, C_V), F32)],
        compiler_params=_params("parallel", "parallel", "arbitrary"),
        name="mla_attention",
    )(q_all, q_rope, kv, k_rope, kv)


def _tril(n, strict=False):
    r = lax.broadcasted_iota(jnp.int32, (n, n), 0)
    c = lax.broadcasted_iota(jnp.int32, (n, n), 1)
    return (r > c) if strict else (r >= c)


def _gla_kernel(q_ref, k_ref, v_ref, og_ref, lo_ref, wg_ref, bg_ref, gn_ref, o_ref, st_sc, *, scale, eps):
    @pl.when(pl.program_id(2) == 0)
    def _():
        st_sc[...] = jnp.zeros(st_sc.shape, F32)

    n = q_ref.shape[0]
    z = jnp.dot(lo_ref[...], wg_ref[...], precision=HIGHEST, preferred_element_type=F32) + bg_ref[...]
    log_a = (jnp.minimum(z, 0.0) - jnp.log1p(jnp.exp(-jnp.abs(z)))) / B_GATE_TAU
    tri = _tril(n)
    b = jnp.dot(tri.astype(F32), log_a, precision=HIGHEST, preferred_element_type=F32)
    b_last = b[n - 1:n, :]
    q = q_ref[...].astype(F32) * scale
    k = k_ref[...].astype(F32)
    v = v_ref[...]
    qe = (q * jnp.exp(b)).astype(BF16)
    ke = (k * jnp.exp(-b)).astype(BF16)
    kl = (k * jnp.exp(b_last - b)).astype(BF16)
    st = st_sc[...]
    o = lax.dot_general(qe, st.astype(BF16), NT_DIMS, preferred_element_type=F32)
    attn = lax.dot_general(qe, ke, NT_DIMS, preferred_element_type=F32)
    attn = jnp.where(tri, attn, 0.0).astype(BF16)
    o = o + jnp.dot(attn, v, preferred_element_type=F32)
    st_sc[...] = st * jnp.exp(b_last) + lax.dot_general(v, kl, TN_DIMS, preferred_element_type=F32)
    o = o * lax.rsqrt(jnp.mean(o * o, axis=-1, keepdims=True) + eps) * gn_ref[...]
    og = og_ref[...].astype(F32)
    o_ref[...] = (o * (og * jax.nn.sigmoid(og))).astype(o_ref.dtype)


def gla(p, p_lo, w_gate, b_gate, gn, batch, seq, cols):
    t = p.shape[0]
    heads = w_gate.shape[1] // B_K_DIM
    ch = GLA_CHUNK
    nc = seq // ch
    row = lambda b, h, c: b * nc + c
    return pl.pallas_call(
        functools.partial(_gla_kernel, scale=B_K_DIM ** -0.5, eps=EPS),
        grid=(batch, heads, nc),
        in_specs=[
            pl.BlockSpec((ch, B_K_DIM), lambda b, h, c: (row(b, h, c), cols["bq"] // B_K_DIM + h)),
            pl.BlockSpec((ch, B_K_DIM), lambda b, h, c: (row(b, h, c), cols["bk"] // B_K_DIM + h)),
            pl.BlockSpec((ch, B_V_DIM), lambda b, h, c: (row(b, h, c), cols["bv"] // B_V_DIM + h)),
            pl.BlockSpec((ch, B_V_DIM), lambda b, h, c: (row(b, h, c), cols["og"] // B_V_DIM + h)),
            pl.BlockSpec((ch, LANES), lambda b, h, c: (row(b, h, c), 0)),
            pl.BlockSpec((LANES, B_K_DIM), lambda b, h, c: (0, h)),
            pl.BlockSpec((1, B_K_DIM), lambda b, h, c: (0, h)),
            pl.BlockSpec((1, B_V_DIM), lambda b, h, c: (0, 0)),
        ],
        out_specs=pl.BlockSpec((ch, B_V_DIM), lambda b, h, c: (row(b, h, c), h)),
        out_shape=jax.ShapeDtypeStruct((t, heads * B_V_DIM), BF16),
        scratch_shapes=[pltpu.VMEM((B_V_DIM, B_K_DIM), F32)],
        compiler_params=_params("parallel", "parallel", "arbitrary"),
        name="gla",
    )(p, p, p, p, p_lo, w_gate, b_gate.reshape(1, -1), gn.reshape(1, -1))


def _rwkv_prep_kernel(pd_ref, mu_ref, w0_ref, w2_ref, a0_ref, a2_ref, g2_ref,
                      r_ref, k_ref, v_ref, wl_ref, a_ref, g_ref, carry_sc, *, width, rw, ra):
    @pl.when(pl.program_id(1) == 0)
    def _():
        carry_sc[...] = jnp.zeros(carry_sc.shape, F32)

    x = pd_ref[...]
    n = x.shape[0]
    first = lax.broadcasted_iota(jnp.int32, x.shape, 0) == 0
    prev = jnp.where(first, carry_sc[...], pltpu.roll(x, 1, 0))
    carry_sc[...] = x[n - 1:n, :]
    xs = x + (prev - x) * mu_ref[...]
    r_ref[...] = xs[:, 0:width]
    k_ref[...] = xs[:, width:2 * width]
    v_ref[...] = xs[:, 2 * width:3 * width]
    c0 = 3 * width
    xw, xa, xg = xs[:, c0:c0 + rw], xs[:, c0 + rw:c0 + rw + ra], xs[:, c0 + rw + ra:]
    zw = w0_ref[...] + jnp.dot(jnp.tanh(xw), w2_ref[...], precision=HIGHEST, preferred_element_type=F32)
    w = -(jnp.maximum(-zw, 0.0) + jnp.log1p(jnp.exp(-jnp.abs(zw)))) - 0.5
    wl_ref[...] = -jnp.exp(w)
    za = a0_ref[...] + jnp.dot(xa, a2_ref[...], precision=HIGHEST, preferred_element_type=F32)
    a_ref[...] = jax.nn.sigmoid(za)
    g_ref[...] = jnp.dot(jax.nn.sigmoid(xg), g2_ref[...], precision=HIGHEST, preferred_element_type=F32)


def rwkv_prep(pd, mu, w0, w2, a0, a2, g2, batch, seq):
    t, cols = pd.shape
    width = w0.shape[-1]
    rw, ra, rg = w2.shape[0], a2.shape[0], g2.shape[0]
    tb = _pick(seq, (256, 128, 64, 32, 16, 8))
    nb = seq // tb
    full = lambda shape: pl.BlockSpec(shape, lambda b, s: (0, 0))
    ospec = pl.BlockSpec((tb, width), lambda b, s: (b * nb + s, 0))
    return pl.pallas_call(
        functools.partial(_rwkv_prep_kernel, width=width, rw=rw, ra=ra),
        grid=(batch, nb),
        in_specs=[pl.BlockSpec((tb, cols), lambda b, s: (b * nb + s, 0)), full((1, cols)), full((1, width)),
                  full((rw, width)), full((1, width)), full((ra, width)), full((rg, width))],
        out_specs=[ospec] * 6,
        out_shape=[jax.ShapeDtypeStruct((t, width), F32)] * 6,
        scratch_shapes=[pltpu.VMEM((1, cols), F32)],
        compiler_params=_params("parallel", "arbitrary"),
        name="rwkv_prep",
    )(pd, mu.reshape(1, cols), w0.reshape(1, width), w2, a0.reshape(1, width), a2, g2)


def _rwkv_scan_kernel(r_ref, k_ref, v_ref, wl_ref, a_ref, g_ref, kk_ref, ka_ref, rk_ref, gw_ref, gb_ref,
                      o_ref, st_sc, *, heads, chunk, gn_eps):
    @pl.when(pl.program_id(2) == 0)
    def _():
        st_sc[...] = jnp.zeros(st_sc.shape, F32)

    n = chunk
    hw = heads * D_HEAD
    rows = heads * n
    ri = lax.broadcasted_iota(jnp.int32, (rows, hw), 0)
    ci = lax.broadcasted_iota(jnp.int32, (rows, hw), 1)
    same_head = (ri // n) == (ci // D_HEAD)
    rt_i = lax.broadcasted_iota(jnp.int32, (rows, rows), 0)
    ct_i = lax.broadcasted_iota(jnp.int32, (rows, rows), 1)
    strict = (rt_i % n) > (ct_i % n)
    incl = (rt_i % n) >= (ct_i % n)
    eye = (rt_i == ct_i).astype(F32)
    li = lax.broadcasted_iota(jnp.int32, (hw, hw), 0)
    lj = lax.broadcasted_iota(jnp.int32, (hw, hw), 1)
    head_ones = ((li // D_HEAD) == (lj // D_HEAD)).astype(BF16)
    tri_n = _tril(n).astype(F32)

    def to_blk(x):
        return jnp.where(same_head, jnp.concatenate([x] * heads, axis=0), 0.0).astype(BF16)

    def from_blk(y):
        out = y[0:n]
        for h in range(1, heads):
            out = out + y[h * n:(h + 1) * n]
        return out

    def head_sum(x):
        hi = x.astype(BF16)
        lo = (x - hi.astype(F32)).astype(BF16)
        s = jnp.dot(jnp.concatenate([hi, lo], axis=0), head_ones, preferred_element_type=F32)
        return s[:n] + s[n:]

    bdot = functools.partial(jnp.dot, preferred_element_type=F32)
    each = lambda f, *cols: [f(*a) for a in zip(*cols)]
    n_chunks, n_groups = r_ref.shape[0] // n, r_ref.shape[1] // hw
    ids = [(c, gi) for c in range(n_chunks) for gi in range(n_groups)]
    sls = [slice(c * n, (c + 1) * n) for c, _ in ids]
    lns = [slice(gi * hw, (gi + 1) * hw) for _, gi in ids]
    load = lambda ref: [ref[sl, ln] for sl, ln in zip(sls, lns)]
    par = lambda ref: [ref[:, ln] for ln in lns]
    r, k, v, wl, a_s = load(r_ref), load(k_ref), load(v_ref), load(wl_ref), load(a_ref)
    cum = each(lambda x: jnp.dot(tri_n, x, precision=HIGHEST, preferred_element_type=F32), wl)
    kk = each(lambda x, w: x * w, k, par(kk_ref))
    ksq = each(lambda x: head_sum(x * x), kk)
    kk = each(lambda x, s: x * lax.rsqrt(jnp.maximum(s, 1e-24)), kk, ksq)
    ke = each(lambda x, a, w: x * (1.0 + (a - 1.0) * w), k, a_s, par(ka_ref))
    cum_last = each(lambda x: x[n - 1:n, :], cum)
    e_neg = each(lambda x: jnp.exp(-x), cum)
    e_end = each(lambda x, xl: jnp.exp(xl - x), cum, cum_last)
    kka = each(lambda x, a: x * a, kk, a_s)
    lhs = each(lambda x, cu, w, rr: jnp.concatenate([to_blk(-x * jnp.exp(cu - w)), to_blk(rr * jnp.exp(cu))], axis=0),
               kk, cum, wl, r)
    rhs = each(lambda x, y, e: jnp.concatenate([to_blk(x * e), to_blk(y * e)], axis=0), kka, ke, e_neg)
    ends = each(lambda x, y, e: jnp.concatenate([to_blk(x * e), to_blk(y * e)], axis=0), kka, ke, e_end)
    vb = each(to_blk, v)
    cross = each(lambda a, b: lax.dot_general(a, b, NT_DIMS, preferred_element_type=F32), lhs, rhs)
    nab = each(lambda x: jnp.where(strict, x[:rows, :rows], 0.0), cross)
    aak = each(lambda x: jnp.where(strict, x[:rows, rows:], 0.0).astype(BF16), cross)
    rbk = each(lambda x: jnp.concatenate([jnp.where(incl, x[rows:, :rows], 0.0),
                                          jnp.where(incl, x[rows:, rows:], 0.0)], axis=1).astype(BF16), cross)
    inv = each(lambda x: eye + x, nab)
    pw = each(lambda x: x.astype(BF16), nab)
    for _ in range(max(1, int(math.ceil(math.log2(n))) - 1)):
        pw = each(lambda x: bdot(x, x).astype(BF16), pw)
        inv = each(lambda x, p: x + bdot(x.astype(BF16), p), inv, pw)
    inv = each(lambda x: x.astype(BF16), inv)
    akv = each(bdot, aak, vb)
    bonus = each(lambda rr, x, w, vv: head_sum(rr * x * w) * vv, r, ke, par(rk_ref), v)
    decay_end = each(jnp.exp, cum_last)
    ys = [None] * len(ids)
    for c in range(n_chunks):
        idx = [i for i, (cc, _) in enumerate(ids) if cc == c]
        take = lambda col: [col[i] for i in idx]
        s0 = [st_sc[ids[i][1]] for i in idx]
        ls = each(lambda a, s: lax.dot_general(a, s.astype(BF16), NT_DIMS, preferred_element_type=F32), take(lhs), s0)
        u = each(lambda m, x, y: bdot(m, (x[:rows] + y).astype(BF16)), take(inv), ls, take(akv))
        uv = each(lambda x, y: jnp.concatenate([x.astype(BF16), y], axis=0), u, take(vb))
        y = each(lambda x, m, z: from_blk(x[rows:] + bdot(m, z)), ls, take(rbk), uv)
        new = each(lambda s, d, z, e: s * d + lax.dot_general(z, e, TN_DIMS, preferred_element_type=F32),
                   s0, take(decay_end), uv, take(ends))
        for i, st, yy in zip(idx, new, y):
            st_sc[ids[i][1]] = st
            ys[i] = yy
    mu = each(lambda x: head_sum(x) * (1.0 / D_HEAD), ys)
    yc = each(lambda x, m: x - m, ys, mu)
    var = each(lambda x: head_sum(x * x) * (1.0 / D_HEAD), yc)
    yn = each(lambda x, s, w, b: x * lax.rsqrt(s + gn_eps) * w + b, yc, var, par(gw_ref), par(gb_ref))
    for sl, ln, x, b in zip(sls, lns, yn, bonus):
        o_ref[sl, ln] = ((x + b) * g_ref[sl, ln]).astype(o_ref.dtype)


def rwkv_scan(r, k, v, wl, a, g, k_k, k_a, r_k, gn_w, gn_b, batch, seq):
    t, width = r.shape
    hp = RWKV_HEADS_PER_STEP
    hw = hp * D_HEAD
    gw = _pick(width, (RWKV_GROUPS_PER_STEP * hw, hw))
    ch = RWKV_CHUNK
    tb = _pick(seq, (2 * ch, ch))
    nc = seq // tb
    xspec = pl.BlockSpec((tb, gw), lambda b, h, c: (b * nc + c, h))
    pspec = pl.BlockSpec((1, gw), lambda b, h, c: (0, h))
    vec = lambda p: p.reshape(1, width).astype(F32)
    return pl.pallas_call(
        functools.partial(_rwkv_scan_kernel, heads=hp, chunk=ch, gn_eps=D_GN_EPS),
        grid=(batch, width // gw, nc),
        in_specs=[xspec] * 6 + [pspec] * 5,
        out_specs=xspec,
        out_shape=jax.ShapeDtypeStruct((t, width), BF16),
        scratch_shapes=[pltpu.VMEM((gw // hw, hw, hw), F32)],
        compiler_params=_params("parallel", "parallel", "arbitrary"),
        name="rwkv_scan",
    )(r, k, v, wl, a, g, vec(k_k), vec(k_a), vec(r_k), vec(gn_w), vec(gn_b))


def _extract_top(work, dst_ref, count):
    for r in range(count):
        m = jnp.max(work, axis=0, keepdims=True)
        dst_ref[r:r + 1, :] = m
        work = jnp.where(work >= m, -jnp.inf, work)


def _peer_topk_kernel(q_ref, keys_ref, thr_ref, f1_ref, s2_ref, e2_ref, a1_sc, a2_sc, cand_sc, top_sc):
    kk = P_TOPK
    for h in range(P_HEADS):
        sc = []
        for half in range(2):
            qh = q_ref[:, (2 * h + half) * LANES:(2 * h + half + 1) * LANES]
            sc.append(lax.dot_general(keys_ref[h, half], qh, NT_DIMS, precision=HIGHEST,
                                      preferred_element_type=F32))
        _extract_top(sc[0], a1_sc, kk)
        _extract_top(sc[1], a2_sc, kk)
        a2 = a2_sc[...]
        for i in range(kk):
            cand_sc[i * kk:(i + 1) * kk, :] = a1_sc[i:i + 1, :] + a2
        _extract_top(cand_sc[...], top_sc, kk)
        top = top_sc[...]
        best = top[0:1, :]
        tau = top[kk - 1:kk, :]
        zsum = jnp.sum(jnp.exp(top - best), axis=0, keepdims=True)
        thr = jnp.full(sc[0].shape, jnp.inf, F32)
        for j in range(kk):
            bj = a2_sc[j:j + 1, :]
            thr = jnp.where(sc[0] + bj >= tau, bj, thr)
        thr_ref[h] = thr
        s2_ref[h] = sc[1]
        f1_ref[h] = jnp.exp(sc[0] - a1_sc[0:1, :]) / zsum
        e2_ref[h] = jnp.exp(sc[1] - a2_sc[0:1, :])


def peer_topk(q, keys):
    t = q.shape[0]
    tt = _pick(t, (256, 128))
    big = pl.BlockSpec((P_HEADS, P_NKEYS, tt), lambda i: (0, 0, i))
    big_shape = jax.ShapeDtypeStruct((P_HEADS, P_NKEYS, t), F32)
    return pl.pallas_call(
        _peer_topk_kernel,
        grid=(t // tt,),
        in_specs=[pl.BlockSpec((tt, q.shape[1]), lambda i: (i, 0)),
                  pl.BlockSpec(keys.shape, lambda i: (0, 0, 0, 0))],
        out_specs=[big, big, big, big],
        out_shape=[big_shape] * 4,
        scratch_shapes=[pltpu.VMEM((P_TOPK, tt), F32), pltpu.VMEM((P_TOPK, tt), F32),
                        pltpu.VMEM((P_TOPK * P_TOPK, tt), F32), pltpu.VMEM((P_TOPK, tt), F32)],
        compiler_params=_params("parallel"),
        name="peer_topk",
    )(q, keys)


def _peer_dense_kernel(x_ref, u_ref, v_ref, thr_ref, f1_ref, s2_ref, e2_ref, o_ref, *, rows, chains):
    @pl.when(pl.program_id(1) == 0)
    def _():
        o_ref[...] = jnp.zeros(o_ref.shape, F32)

    tc = x_ref.shape[0] // chains
    hr = rows // 2
    dh = o_ref.shape[1] // 2
    tok = lambda ch: slice(ch * tc, (ch + 1) * tc)

    def activation(ch, half):
        es = slice(half * hr * P_NKEYS, (half + 1) * hr * P_NKEYS)
        return lax.dot_general(u_ref[es, :], x_ref[tok(ch), :], NT_DIMS, preferred_element_type=F32)

    def gated(ch, half, act):
        ts = tok(ch)
        act = 0.5 * act * (1.0 + lax.erf(act * (2.0 ** -0.5)))
        parts = []
        for r in range(hr):
            row = half * hr + r
            coef = None
            for h in range(P_HEADS):
                sel = s2_ref[h, :, ts] >= thr_ref[h, row:row + 1, ts]
                c = jnp.where(sel, e2_ref[h, :, ts] * f1_ref[h, row:row + 1, ts], 0.0)
                coef = c if coef is None else coef + c
            parts.append((coef * act[r * P_NKEYS:(r + 1) * P_NKEYS, :]).astype(BF16))
        return parts

    def project(ch, half, w):
        cs = slice(half * dh, (half + 1) * dh)
        o_ref[tok(ch), cs] += lax.dot_general(w, v_ref[:, cs], TN_DIMS, preferred_element_type=F32)

    acts = [activation(0, 0), activation(0, 1)]
    w_prev = None
    for ch in range(chains):
        nxt, parts = [], []
        for half in range(2):
            parts += gated(ch, half, acts[half])
            if ch + 1 < chains:
                nxt.append(activation(ch + 1, half))
            if w_prev is not None:
                project(ch - 1, half, w_prev)
        w_prev = jnp.concatenate(parts, axis=0)
        acts = nxt
    project(chains - 1, 0, w_prev)
    project(chains - 1, 1, w_prev)


def peer_dense(x, u, v, thr, f1, s2, e2):
    t, d = x.shape
    n = u.shape[0]
    tt = _pick(t, (1024, 512, 256, 128))
    chains = 2 if tt % 256 == 0 else 1
    rows = 4
    tn = rows * P_NKEYS
    thr_r = thr.reshape(P_HEADS, P_NKEYS // rows, rows, t)
    f1r = f1.reshape(P_HEADS, P_NKEYS // rows, rows, t)
    once = pl.Buffered(1)
    rowspec = pl.BlockSpec((P_HEADS, None, rows, tt), lambda i, e: (0, e, 0, i))
    fullspec = pl.BlockSpec((P_HEADS, P_NKEYS, tt), lambda i, e: (0, 0, i), pipeline_mode=once)
    return pl.pallas_call(
        functools.partial(_peer_dense_kernel, rows=rows, chains=chains),
        grid=(t // tt, n // tn),
        in_specs=[pl.BlockSpec((tt, d), lambda i, e: (i, 0), pipeline_mode=once),
                  pl.BlockSpec((tn, d), lambda i, e: (e, 0)),
                  pl.BlockSpec((tn, d), lambda i, e: (e, 0)),
                  rowspec, rowspec, fullspec, fullspec],
        out_specs=pl.BlockSpec((tt, d), lambda i, e: (i, 0), pipeline_mode=once),
        out_shape=jax.ShapeDtypeStruct((t, d), F32),
        compiler_params=_params("parallel", "arbitrary"),
        name="peer_dense",
    )(x, u, v, thr_r, f1r, s2, e2)


def peer_ffn(hn, w_q, keys, u, v):
    q = matmul([hn], [w_q.astype(BF16)], F32)
    thr, f1, s2, e2 = peer_topk(q, keys)
    return peer_dense(hn, u.astype(BF16), v.astype(BF16), thr, f1, s2, e2)


def even_mixer(hn, w_in, w_out, lam_params, subln_gain, lam_init, w_gate, b_gate, gla_norm, batch, seq):
    na = w_out.shape[0] // 2
    a_heads = na // A_V_DIM
    b_heads = na // B_V_DIM
    sizes = (2 * a_heads * A_QK_DIM, 2 * a_heads * A_QK_DIM, na, b_heads * B_K_DIM, b_heads * B_K_DIM, na,
             w_gate.shape[0], na)
    names = ("aq", "ak", "av", "bq", "bk", "bv", "lo", "og")
    start, off = {}, 0
    for nm, sz in zip(names, sizes):
        start[nm] = off
        off += sz
    w_main = jnp.concatenate([w_in[:, :start["lo"]], w_in[:, start["og"]:]], axis=1).astype(BF16)
    cols = dict(start)
    cols["og"] = start["lo"]
    w_lo = _pad_cols(w_in[:, start["lo"]:start["og"]], LANES).astype(BF16)
    p = matmul([hn], [w_main], BF16)
    p_lo = matmul([hn], [w_lo], F32)
    qk = rope(p, rope_tables(seq, A_ROT), seq, width=start["av"], col_block=0, half=A_ROT // 2)
    lp = lam_params.astype(F32)
    lam = jnp.exp(jnp.sum(lp[0] * lp[1])) - jnp.exp(jnp.sum(lp[2] * lp[3])) + lam_init
    o_a = diff_attention(qk, p, lam, subln_gain, lam_init, batch, seq, start["av"])
    w_gate_p = jnp.pad(w_gate.astype(F32), ((0, LANES - w_gate.shape[0]), (0, 0)))
    o_b = gla(p, p_lo, w_gate_p, b_gate.astype(F32), gla_norm.astype(F32), batch, seq, cols)
    w_out = w_out.astype(BF16)
    return matmul([o_a, o_b], [w_out[:na], w_out[na:]], F32)


def odd_mixer(hn, w_in, w_out, q_norm, kv_norm, w_uq, w_ukv, mu, w0, w2, a0, a2, g2,
              k_k, k_a, r_k, gn_w, gn_b, batch, seq):
    q_rank, kv_rank = q_norm.shape[0], kv_norm.shape[0]
    width = w0.shape[0]
    rw, ra, rg = w2.shape[0], a2.shape[0], g2.shape[0]
    c_cols = q_rank + kv_rank + C_ROPE
    heads = w_uq.shape[1] // (C_NOPE + C_ROPE)
    c_pad = -(-(q_rank + kv_rank + LANES) // 256) * 256
    w_c = _pad_cols(w_in[:, :c_cols], c_pad).astype(BF16)
    pad_l = lambda m: -(-m // LANES) * LANES
    d0 = c_cols
    segs, mus, off = [], [], d0
    for sz in (width, width, width, rw, ra, rg):
        segs.append(_pad_cols(w_in[:, off:off + sz], pad_l(sz)))
        mus.append(jnp.pad(mu[off - d0:off - d0 + sz], (0, pad_l(sz) - sz)))
        off += sz
    w_d = jnp.concatenate(segs, axis=1).astype(BF16)
    mu_d = jnp.concatenate(mus).astype(F32)
    pad_r = lambda w: jnp.pad(w.astype(F32), ((0, pad_l(w.shape[0]) - w.shape[0]), (0, 0)))

    p_c = matmul([hn], [w_c], F32)
    p_d = matmul([hn], [w_d], F32)

    cq = add_rmsnorm([p_c], q_norm, BF16, width=q_rank, col_block=0)
    ckv = add_rmsnorm([p_c], kv_norm, BF16, width=kv_rank, col_block=q_rank // kv_rank)
    wq3 = w_uq.reshape(q_rank, heads, C_NOPE + C_ROPE)
    wq_nope = wq3[:, :, :C_NOPE].reshape(q_rank, heads * C_NOPE)
    wq_rope = jnp.pad(wq3[:, :, C_NOPE:], ((0, 0), (0, 0), (0, LANES - C_ROPE))).reshape(q_rank, heads * LANES)
    q_all = matmul([cq], [jnp.concatenate([wq_nope, wq_rope], axis=1).astype(BF16)], BF16)
    wkv3 = w_ukv.reshape(kv_rank, heads, C_NOPE + C_V)
    w_kv = jnp.concatenate([wkv3[:, :, :C_NOPE].reshape(kv_rank, heads * C_NOPE),
                            wkv3[:, :, C_NOPE:].reshape(kv_rank, heads * C_V)], axis=1).astype(BF16)
    kv = matmul([ckv], [w_kv], BF16)
    tables = rope_tables(seq, C_ROPE)
    q_rope = rope(q_all, tables, seq, width=heads * LANES, col_block=1, half=C_ROPE // 2)
    k_rope = rope(p_c, tables, seq, width=LANES, col_block=(q_rank + kv_rank) // LANES, half=C_ROPE // 2)
    o_c = mla_attention(q_all, q_rope, kv, k_rope, batch, seq)

    r, k, v, wl, a, g = rwkv_prep(p_d, mu_d, w0.astype(F32), pad_r(w2), a0.astype(F32), pad_r(a2), pad_r(g2),
                                  batch, seq)
    o_d = rwkv_scan(r, k, v, wl, a, g, k_k, k_a, r_k, gn_w, gn_b, batch, seq)
    nc = o_c.shape[1]
    w_out = w_out.astype(BF16)
    return matmul([o_c, o_d], [w_out[:nc], w_out[nc:]], F32)


def kernel(x, norm_mix, norm_ffn, norm_final, even_w_in, even_w_out, diff_lambda, diff_subln, gla_w_gate, gla_b_gate, gla_norm, odd_w_in, odd_w_out, mla_q_norm, mla_kv_norm, mla_w_uq, mla_w_ukv, rwkv_mu, rwkv_w0, rwkv_w2, rwkv_a0, rwkv_a2, rwkv_g2, rwkv_k_k, rwkv_k_a, rwkv_r_k, rwkv_gn_w, rwkv_gn_b, peer_w_q, peer_keys, peer_u, peer_v):
    batch, seq, d = x.shape
    depth = norm_mix.shape[0]
    h = x.reshape(batch * seq, d)
    pending = []
    for layer in range(depth):
        j = layer // 2
        if pending:
            h, hn = add_rmsnorm([h] + pending, norm_mix[layer], BF16, emit_sum=True)
        else:
            hn = add_rmsnorm([h], norm_mix[layer], BF16)
        if layer % 2 == 0:
            lam_init = 0.8 - 0.6 * math.exp(-0.3 * layer)
            mix = even_mixer(hn, even_w_in[j], even_w_out[j], diff_lambda[j], diff_subln[j], lam_init,
                             gla_w_gate[j], gla_b_gate[j], gla_norm[j], batch, seq)
        else:
            mix = odd_mixer(hn, odd_w_in[j], odd_w_out[j], mla_q_norm[j], mla_kv_norm[j], mla_w_uq[j],
                            mla_w_ukv[j], rwkv_mu[j], rwkv_w0[j], rwkv_w2[j], rwkv_a0[j], rwkv_a2[j],
                            rwkv_g2[j], rwkv_k_k[j], rwkv_k_a[j], rwkv_r_k[j], rwkv_gn_w[j], rwkv_gn_b[j],
                            batch, seq)
        h, hn2 = add_rmsnorm([h, mix], norm_ffn[layer], BF16, emit_sum=True)
        pending = [peer_ffn(hn2, peer_w_q[layer], peer_keys[layer], peer_u[layer], peer_v[layer])]
    out = add_rmsnorm([h] + pending, norm_final, F32)
    return out.reshape(batch, seq, d).astype(x.dtype)
```

```python
import functools
import math

import jax
import jax.numpy as jnp
from jax import lax
from jax.experimental import pallas as pl
from jax.experimental.pallas import tpu as pltpu

F32 = jnp.float32
BF16 = jnp.bfloat16
FP8 = jnp.float8_e4m3fn
FP8_TARGET_MAX = 256.0
HIGHEST = lax.Precision.HIGHEST

LANES = 128
VMEM_LIMIT = 56 * 1024 * 1024

EPS = 1e-6
ROPE_THETA = 500000.0
A_QK_DIM = 128
A_V_DIM = 256
A_ROT = A_QK_DIM // 4
B_V_DIM = 512
B_K_DIM = 256
B_GATE_TAU = 16.0
GLA_CHUNK = 64
C_NOPE = 128
C_ROPE = 64
C_V = 128
D_HEAD = 64
D_GN_EPS = 64e-5
RWKV_CHUNK = 64
ATTN_BLOCKS = (1024, 512, 256, 128)
ATTN_SUB = 4
RWKV_HEADS_PER_STEP = 4
RWKV_GROUPS_PER_STEP = 4
P_HEADS = 8
P_NKEYS = 128
P_TOPK = 16

NT_DIMS = (((1,), (1,)), ((), ()))
TN_DIMS = (((0,), (0,)), ((), ()))


def _params(*sem):
    return pltpu.CompilerParams(dimension_semantics=sem, vmem_limit_bytes=VMEM_LIMIT)


def _pick(n, cands):
    for c in cands:
        if n % c == 0:
            return c
    raise ValueError(f"no tile in {cands} divides {n}")


def _pad_cols(w, n):
    return jnp.pad(w, ((0, 0), (0, n - w.shape[1])))


def _norm_kernel(*refs, eps, n_add, emit_sum):
    xs = refs[:n_add]
    g_ref = refs[n_add]
    outs = refs[n_add + 1:]
    x = xs[0][...].astype(F32)
    for r in xs[1:]:
        x = x + r[...].astype(F32)
    y = x * lax.rsqrt(jnp.mean(x * x, axis=-1, keepdims=True) + eps) * g_ref[...]
    if emit_sum:
        outs[0][...] = x
        outs[1][...] = y.astype(outs[1].dtype)
    else:
        outs[0][...] = y.astype(outs[0].dtype)


def add_rmsnorm(xs, g, out_dtype, *, emit_sum=False, width=None, col_block=0, eps=EPS):
    t = xs[0].shape[0]
    width = width or xs[0].shape[1]
    tr = _pick(t, (128, 64, 32, 16, 8))
    spec = pl.BlockSpec((tr, width), lambda i: (i, col_block))
    ospec = pl.BlockSpec((tr, width), lambda i: (i, 0))
    out_shape = [jax.ShapeDtypeStruct((t, width), out_dtype)]
    out_specs = [ospec]
    if emit_sum:
        out_shape = [jax.ShapeDtypeStruct((t, width), F32)] + out_shape
        out_specs = [ospec, ospec]
    res = pl.pallas_call(
        functools.partial(_norm_kernel, eps=eps, n_add=len(xs), emit_sum=emit_sum),
        grid=(t // tr,),
        in_specs=[spec] * len(xs) + [pl.BlockSpec((1, width), lambda i: (0, 0))],
        out_specs=out_specs,
        out_shape=out_shape,
        compiler_params=_params("parallel"),
        name="add_rmsnorm",
    )(*xs, g.reshape(1, width).astype(F32))
    return res if emit_sum else res[0]


def _mm_kernel(*refs, n_in):
    o_ref = refs[2 * n_in]
    acc = jnp.dot(refs[0][...], refs[n_in][...], preferred_element_type=F32)
    for i in range(1, n_in):
        acc = acc + jnp.dot(refs[i][...], refs[n_in + i][...], preferred_element_type=F32)
    o_ref[...] = acc.astype(o_ref.dtype)


def matmul(xs, ws, out_dtype):
    m = xs[0].shape[0]
    n = ws[0].shape[1]
    tm = _pick(m, (1024, 512, 256, 128, 64, 32, 16))
    tn = _pick(n, (512, 768, 640, 256, 128))
    in_specs = [pl.BlockSpec((tm, x.shape[1]), lambda i, j: (i, 0)) for x in xs]
    in_specs += [pl.BlockSpec((w.shape[0], tn), lambda i, j: (0, j)) for w in ws]
    return pl.pallas_call(
        functools.partial(_mm_kernel, n_in=len(xs)),
        grid=(m // tm, n // tn),
        in_specs=in_specs,
        out_specs=pl.BlockSpec((tm, tn), lambda i, j: (i, j)),
        out_shape=jax.ShapeDtypeStruct((m, n), out_dtype),
        compiler_params=_params("parallel", "arbitrary"),
        name="matmul",
    )(*xs, *ws)


def rope_tables(seq, rot_dim):
    half = rot_dim // 2
    inv = 1.0 / (ROPE_THETA ** (jnp.arange(0, rot_dim, 2, dtype=F32) / rot_dim))
    ang = jnp.arange(seq, dtype=F32)[:, None] * inv[None, :]
    cos, sin = jnp.cos(ang), jnp.sin(ang)
    z = jnp.zeros((seq, LANES - 2 * half), F32)
    zh = jnp.zeros((seq, half), F32)
    c = jnp.concatenate([cos, cos, jnp.ones_like(z)], axis=1)
    s_up = jnp.concatenate([zh, sin, z], axis=1)
    s_dn = jnp.concatenate([-sin, zh, z], axis=1)
    return c, s_up, s_dn


def _rope_kernel(x_ref, c_ref, su_ref, sd_ref, o_ref, *, half, groups):
    c, su, sd = c_ref[...], su_ref[...], sd_ref[...]
    for g in range(groups):
        sl = slice(g * LANES, (g + 1) * LANES)
        x = x_ref[:, sl].astype(F32)
        y = x * c + pltpu.roll(x, half, 1) * su + pltpu.roll(x, LANES - half, 1) * sd
        o_ref[:, sl] = y.astype(o_ref.dtype)


def rope(x, tables, seq, *, width, col_block, half):
    t = x.shape[0]
    tr = _pick(seq, (256, 128, 64, 32, 16))
    nb = seq // tr
    tspec = pl.BlockSpec((tr, LANES), lambda i: (i % nb, 0))
    return pl.pallas_call(
        functools.partial(_rope_kernel, half=half, groups=width // LANES),
        grid=(t // tr,),
        in_specs=[pl.BlockSpec((tr, width), lambda i: (i, col_block)), tspec, tspec, tspec],
        out_specs=pl.BlockSpec((tr, width), lambda i: (i, 0)),
        out_shape=jax.ShapeDtypeStruct((t, width), BF16),
        compiler_params=_params("parallel"),
        name="rope",
    )(x, *tables)


LOG2E = 1.4426950408889634


def _scaled_q(q, scale):
    return (q.astype(F32) * (scale * LOG2E)).astype(BF16)


def _softmax_chains(chains):
    s = [lax.dot_general(q, k, NT_DIMS, preferred_element_type=F32) for q, k, *_ in chains]
    def masked(x, tail):
        if tail is None:
            return x
        w = tail.shape[1]
        last = jnp.where(tail, x[:, x.shape[1] - w:], -jnp.inf)
        return last if w == x.shape[1] else jnp.concatenate([x[:, :x.shape[1] - w], last], axis=1)

    s = [masked(x, c[6]) for x, c in zip(s, chains)]
    m_prev = [c[3][...] for c in chains]
    m_new = [jnp.maximum(mp, jnp.max(x, axis=-1, keepdims=True)) for mp, x in zip(m_prev, s)]
    alpha = [jnp.exp2(mp - mn) for mp, mn in zip(m_prev, m_new)]
    p = [jnp.exp2(x - mn) for x, mn in zip(s, m_new)]
    pv = [jnp.dot(x.astype(c[2].dtype), c[2], preferred_element_type=F32) for x, c in zip(p, chains)]
    for c, a, x, y, mn in zip(chains, alpha, p, pv, m_new):
        c[4][...] = a * c[4][...] + jnp.sum(x, axis=-1, keepdims=True)
        c[5][...] = a * c[5][...] + y
        c[3][...] = mn


def _causal_sweep(i, tq, step):
    def body(j, carry):
        step(pl.ds(pl.multiple_of(j * tq, tq), tq), False)
        return carry
    lax.fori_loop(0, i, body, 0)
    step(pl.ds(pl.multiple_of(i * tq, tq), tq), True)


def _row_groups(q, k, v, m_sc, l_sc, acc_sc, diagonal):
    rs = q.shape[0] // ATTN_SUB
    chains = []
    tri = _tril(rs)
    for g in range(ATTN_SUB):
        qg = q[g * rs:(g + 1) * rs]
        if diagonal:
            n = (g + 1) * rs
            chains.append((qg, k[:n], v[:n], m_sc.at[g], l_sc.at[g], acc_sc.at[g], tri))
        else:
            chains.append((qg, k, v, m_sc.at[g], l_sc.at[g], acc_sc.at[g], None))
    return chains


def _diff_attn_kernel(lam_ref, q_ref, k_ref, v_ref, g_ref, o_ref, m_sc, l_sc, acc_sc, *, scale, tq, out_scale, eps):
    i = pl.program_id(2)
    m_sc[...] = jnp.full(m_sc.shape, -jnp.inf, F32)
    l_sc[...] = jnp.zeros(l_sc.shape, F32)
    acc_sc[...] = jnp.zeros(acc_sc.shape, F32)
    qs = [_scaled_q(q_ref[:, mp * A_QK_DIM:(mp + 1) * A_QK_DIM], scale) for mp in range(2)]

    def step(rows, diagonal):
        v = v_ref[rows, :]
        chains = []
        for mp in range(2):
            k = k_ref[rows, mp * A_QK_DIM:(mp + 1) * A_QK_DIM]
            chains += _row_groups(qs[mp], k, v, m_sc.at[mp], l_sc.at[mp], acc_sc.at[mp], diagonal)
        _softmax_chains(chains)

    _causal_sweep(i, tq, step)
    rs = tq // ATTN_SUB
    for g in range(ATTN_SUB):
        o = acc_sc[0, g] / l_sc[0, g] - lam_ref[0] * (acc_sc[1, g] / l_sc[1, g])
        o = o * lax.rsqrt(jnp.mean(o * o, axis=-1, keepdims=True) + eps) * g_ref[...] * out_scale
        o_ref[g * rs:(g + 1) * rs, :] = o.astype(o_ref.dtype)


def diff_attention(qk, p, lam, subln_gain, lam_init, batch, seq, v_col):
    t = qk.shape[0]
    heads = qk.shape[1] // (4 * A_QK_DIM)
    hw = 2 * A_QK_DIM
    tq = _pick(seq, ATTN_BLOCKS)
    rs = tq // ATTN_SUB
    nq = seq // tq
    vb = v_col // A_V_DIM
    kern = functools.partial(_diff_attn_kernel, scale=A_QK_DIM ** -0.5, tq=tq, out_scale=1.0 - lam_init, eps=EPS)
    return pl.pallas_call(
        kern,
        grid=(batch, heads, nq),
        in_specs=[
            pl.BlockSpec(memory_space=pltpu.SMEM),
            pl.BlockSpec((tq, hw), lambda b, h, i: (b * nq + i, h)),
            pl.BlockSpec((seq, hw), lambda b, h, i: (b, heads + h)),
            pl.BlockSpec((seq, A_V_DIM), lambda b, h, i: (b, vb + h)),
            pl.BlockSpec((1, A_V_DIM), lambda b, h, i: (0, 0)),
        ],
        out_specs=pl.BlockSpec((tq, A_V_DIM), lambda b, h, i: (b * nq + i, h)),
        out_shape=jax.ShapeDtypeStruct((t, heads * A_V_DIM), BF16),
        scratch_shapes=[pltpu.VMEM((2, ATTN_SUB, rs, 1), F32), pltpu.VMEM((2, ATTN_SUB, rs, 1), F32),
                        pltpu.VMEM((2, ATTN_SUB, rs, A_V_DIM), F32)],
        compiler_params=_params("parallel", "parallel", "arbitrary"),
        name="diff_attention",
    )(lam.reshape(1).astype(F32), qk, qk, p, subln_gain.reshape(1, A_V_DIM).astype(F32))


def _mla_attn_kernel(qn_ref, qr_ref, kn_ref, kr_ref, v_ref, o_ref, m_sc, l_sc, acc_sc, *, scale, tq):
    i = pl.program_id(2)
    m_sc[...] = jnp.full(m_sc.shape, -jnp.inf, F32)
    l_sc[...] = jnp.zeros(l_sc.shape, F32)
    acc_sc[...] = jnp.zeros(acc_sc.shape, F32)
    q = _scaled_q(jnp.concatenate([qn_ref[...], qr_ref[...]], axis=-1), scale)

    def step(rows, diagonal):
        k = jnp.concatenate([kn_ref[rows, :], kr_ref[rows, :]], axis=-1)
        _softmax_chains(_row_groups(q, k, v_ref[rows, :], m_sc, l_sc, acc_sc, diagonal))

    _causal_sweep(i, tq, step)
    rs = tq // ATTN_SUB
    for g in range(ATTN_SUB):
        o_ref[g * rs:(g + 1) * rs, :] = (acc_sc[g] / l_sc[g]).astype(o_ref.dtype)


def mla_attention(q_all, q_rope, kv, k_rope, batch, seq):
    t = kv.shape[0]
    heads = kv.shape[1] // (C_NOPE + C_V)
    tq = _pick(seq, ATTN_BLOCKS)
    rs = tq // ATTN_SUB
    nq = seq // tq
    qmap = lambda b, h, i: (b * nq + i, h)
    return pl.pallas_call(
        functools.partial(_mla_attn_kernel, scale=(C_NOPE + C_ROPE) ** -0.5, tq=tq),
        grid=(batch, heads, nq),
        in_specs=[
            pl.BlockSpec((tq, C_NOPE), qmap),
            pl.BlockSpec((tq, LANES), qmap),
            pl.BlockSpec((seq, C_NOPE), lambda b, h, i: (b, h)),
            pl.BlockSpec((seq, LANES), lambda b, h, i: (b, 0)),
            pl.BlockSpec((seq, C_V), lambda b, h, i: (b, heads + h)),
        ],
        out_specs=pl.BlockSpec((tq, C_V), qmap),
        out_shape=jax.ShapeDtypeStruct((t, heads * C_V), BF16),
        scratch_shapes=[pltpu.VMEM((ATTN_SUB, rs, 1), F32), pltpu.VMEM((ATTN_SUB, rs, 1), F32),
                        pltpu.VMEM((ATTN_SUB, rs, C_V), F32)],
        compiler_params=_params("parallel", "parallel", "arbitrary"),
        name="mla_attention",
    )(q_all, q_rope, kv, k_rope, kv)


def _tril(n, strict=False):
    r = lax.broadcasted_iota(jnp.int32, (n, n), 0)
    c = lax.broadcasted_iota(jnp.int32, (n, n), 1)
    return (r > c) if strict else (r >= c)


def _gla_kernel(q_ref, k_ref, v_ref, og_ref, lo_ref, wg_ref, bg_ref, gn_ref, o_ref, st_sc, *, scale, eps, chunk, heads):
    @pl.when(pl.program_id(1) == 0)
    def _():
        st_sc[...] = jnp.zeros(st_sc.shape, F32)

    n = chunk
    n_chunks = q_ref.shape[0] // n
    dk, dv = B_K_DIM, B_V_DIM
    each = lambda f, *cols: [f(*a) for a in zip(*cols)]
    tri = _tril(n)
    trif = tri.astype(F32)
    rows = [slice(c * n, (c + 1) * n) for c in range(n_chunks)]
    z = [jnp.dot(lo_ref[r, :], wg_ref[...], precision=HIGHEST, preferred_element_type=F32) + bg_ref[...] for r in rows]
    log_a = each(lambda x: (jnp.minimum(x, 0.0) - jnp.log1p(jnp.exp(-jnp.abs(x)))) / B_GATE_TAU, z)
    b_all = each(lambda x: jnp.dot(trif, x, precision=HIGHEST, preferred_element_type=F32), log_a)
    ids = [(c, h) for c in range(n_chunks) for h in range(heads)]
    kl = [slice(h * dk, (h + 1) * dk) for _, h in ids]
    vl = [slice(h * dv, (h + 1) * dv) for _, h in ids]
    rw = [rows[c] for c, _ in ids]
    b = [b_all[c][:, s] for (c, _), s in zip(ids, kl)]
    b_last = each(lambda x: x[n - 1:n, :], b)
    q = [q_ref[r, s].astype(F32) * scale for r, s in zip(rw, kl)]
    k = [k_ref[r, s].astype(F32) for r, s in zip(rw, kl)]
    v = [v_ref[r, s] for r, s in zip(rw, vl)]
    qe = each(lambda x, bb: (x * jnp.exp(bb)).astype(BF16), q, b)
    ke = each(lambda x, bb: (x * jnp.exp(-bb)).astype(BF16), k, b)
    kend = each(lambda x, bb, bl: (x * jnp.exp(bl - bb)).astype(BF16), k, b, b_last)
    attn = each(lambda a, c: jnp.where(tri, lax.dot_general(a, c, NT_DIMS, preferred_element_type=F32), 0.0).astype(BF16),
                qe, ke)
    o_intra = each(lambda a, vv: jnp.dot(a, vv, preferred_element_type=F32), attn, v)
    decay = each(jnp.exp, b_last)
    outs = [None] * len(ids)
    for c in range(n_chunks):
        idx = [i for i, (cc, _) in enumerate(ids) if cc == c]
        st = [st_sc[ids[i][1]] for i in idx]
        o_inter = [lax.dot_general(qe[i], s.astype(BF16), NT_DIMS, preferred_element_type=F32) for i, s in zip(idx, st)]
        new = [s * decay[i] + lax.dot_general(v[i], kend[i], TN_DIMS, preferred_element_type=F32)
               for i, s in zip(idx, st)]
        for i, s_new, oi in zip(idx, new, o_inter):
            st_sc[ids[i][1]] = s_new
            outs[i] = oi + o_intra[i]
    for r, s, o in zip(rw, vl, outs):
        o = o * lax.rsqrt(jnp.mean(o * o, axis=-1, keepdims=True) + eps) * gn_ref[...]
        og = og_ref[r, s].astype(F32)
        o_ref[r, s] = (o * (og * jax.nn.sigmoid(og))).astype(o_ref.dtype)


def gla(p, p_lo, w_gate, b_gate, gn, batch, seq, cols):
    t = p.shape[0]
    heads = w_gate.shape[1] // B_K_DIM
    ch = GLA_CHUNK
    tb = _pick(seq, (2 * ch, ch))
    nb = seq // tb
    kw, vw = heads * B_K_DIM, heads * B_V_DIM
    row = lambda b, c: b * nb + c
    return pl.pallas_call(
        functools.partial(_gla_kernel, scale=B_K_DIM ** -0.5, eps=EPS, chunk=ch, heads=heads),
        grid=(batch, nb),
        in_specs=[
            pl.BlockSpec((tb, kw), lambda b, c: (row(b, c), cols["bq"] // kw)),
            pl.BlockSpec((tb, kw), lambda b, c: (row(b, c), cols["bk"] // kw)),
            pl.BlockSpec((tb, vw), lambda b, c: (row(b, c), cols["bv"] // vw)),
            pl.BlockSpec((tb, vw), lambda b, c: (row(b, c), cols["og"] // vw)),
            pl.BlockSpec((tb, LANES), lambda b, c: (row(b, c), 0)),
            pl.BlockSpec((LANES, kw), lambda b, c: (0, 0)),
            pl.BlockSpec((1, kw), lambda b, c: (0, 0)),
            pl.BlockSpec((1, B_V_DIM), lambda b, c: (0, 0)),
        ],
        out_specs=pl.BlockSpec((tb, vw), lambda b, c: (row(b, c), 0)),
        out_shape=jax.ShapeDtypeStruct((t, vw), BF16),
        scratch_shapes=[pltpu.VMEM((heads, B_V_DIM, B_K_DIM), F32)],
        compiler_params=_params("parallel", "arbitrary"),
        name="gla",
    )(p, p, p, p, p_lo, w_gate, b_gate.reshape(1, -1), gn.reshape(1, -1))


def _rwkv_prep_kernel(pd_ref, mu_ref, w0_ref, w2_ref, a0_ref, a2_ref, g2_ref,
                      r_ref, k_ref, v_ref, wl_ref, a_ref, g_ref, carry_sc, *, width, rw, ra):
    @pl.when(pl.program_id(1) == 0)
    def _():
        carry_sc[...] = jnp.zeros(carry_sc.shape, F32)

    x = pd_ref[...]
    n = x.shape[0]
    first = lax.broadcasted_iota(jnp.int32, x.shape, 0) == 0
    prev = jnp.where(first, carry_sc[...], pltpu.roll(x, 1, 0))
    carry_sc[...] = x[n - 1:n, :]
    xs = x + (prev - x) * mu_ref[...]
    r_ref[...] = xs[:, 0:width]
    k_ref[...] = xs[:, width:2 * width]
    v_ref[...] = xs[:, 2 * width:3 * width]
    c0 = 3 * width
    xw, xa, xg = xs[:, c0:c0 + rw], xs[:, c0 + rw:c0 + rw + ra], xs[:, c0 + rw + ra:]
    zw = w0_ref[...] + jnp.dot(jnp.tanh(xw), w2_ref[...], precision=HIGHEST, preferred_element_type=F32)
    w = -(jnp.maximum(-zw, 0.0) + jnp.log1p(jnp.exp(-jnp.abs(zw)))) - 0.5
    wl_ref[...] = -jnp.exp(w)
    za = a0_ref[...] + jnp.dot(xa, a2_ref[...], precision=HIGHEST, preferred_element_type=F32)
    a_ref[...] = jax.nn.sigmoid(za)
    g_ref[...] = jnp.dot(jax.nn.sigmoid(xg), g2_ref[...], precision=HIGHEST, preferred_element_type=F32)


def rwkv_prep(pd, mu, w0, w2, a0, a2, g2, batch, seq):
    t, cols = pd.shape
    width = w0.shape[-1]
    rw, ra, rg = w2.shape[0], a2.shape[0], g2.shape[0]
    tb = _pick(seq, (256, 128, 64, 32, 16, 8))
    nb = seq // tb
    full = lambda shape: pl.BlockSpec(shape, lambda b, s: (0, 0))
    ospec = pl.BlockSpec((tb, width), lambda b, s: (b * nb + s, 0))
    return pl.pallas_call(
        functools.partial(_rwkv_prep_kernel, width=width, rw=rw, ra=ra),
        grid=(batch, nb),
        in_specs=[pl.BlockSpec((tb, cols), lambda b, s: (b * nb + s, 0)), full((1, cols)), full((1, width)),
                  full((rw, width)), full((1, width)), full((ra, width)), full((rg, width))],
        out_specs=[ospec] * 6,
        out_shape=[jax.ShapeDtypeStruct((t, width), F32)] * 6,
        scratch_shapes=[pltpu.VMEM((1, cols), F32)],
        compiler_params=_params("parallel", "arbitrary"),
        name="rwkv_prep",
    )(pd, mu.reshape(1, cols), w0.reshape(1, width), w2, a0.reshape(1, width), a2, g2)


def _rwkv_scan_kernel(r_ref, k_ref, v_ref, wl_ref, a_ref, g_ref, kk_ref, ka_ref, rk_ref, gw_ref, gb_ref,
                      o_ref, st_sc, *, heads, chunk, gn_eps):
    @pl.when(pl.program_id(2) == 0)
    def _():
        st_sc[...] = jnp.zeros(st_sc.shape, F32)

    n = chunk
    hw = heads * D_HEAD
    rows = heads * n
    ri = lax.broadcasted_iota(jnp.int32, (rows, hw), 0)
    ci = lax.broadcasted_iota(jnp.int32, (rows, hw), 1)
    same_head = (ri // n) == (ci // D_HEAD)
    rt_i = lax.broadcasted_iota(jnp.int32, (rows, rows), 0)
    ct_i = lax.broadcasted_iota(jnp.int32, (rows, rows), 1)
    strict = (rt_i % n) > (ct_i % n)
    incl = (rt_i % n) >= (ct_i % n)
    eye = (rt_i == ct_i).astype(F32)
    li = lax.broadcasted_iota(jnp.int32, (hw, hw), 0)
    lj = lax.broadcasted_iota(jnp.int32, (hw, hw), 1)
    head_ones = ((li // D_HEAD) == (lj // D_HEAD)).astype(BF16)
    tri_n = _tril(n).astype(F32)

    def to_blk(x):
        return jnp.where(same_head, jnp.concatenate([x] * heads, axis=0), 0.0).astype(BF16)

    def from_blk(y):
        out = y[0:n]
        for h in range(1, heads):
            out = out + y[h * n:(h + 1) * n]
        return out

    def head_sum(x):
        hi = x.astype(BF16)
        lo = (x - hi.astype(F32)).astype(BF16)
        s = jnp.dot(jnp.concatenate([hi, lo], axis=0), head_ones, preferred_element_type=F32)
        return s[:n] + s[n:]

    bdot = functools.partial(jnp.dot, preferred_element_type=F32)
    each = lambda f, *cols: [f(*a) for a in zip(*cols)]
    n_chunks, n_groups = r_ref.shape[0] // n, r_ref.shape[1] // hw
    ids = [(c, gi) for c in range(n_chunks) for gi in range(n_groups)]
    sls = [slice(c * n, (c + 1) * n) for c, _ in ids]
    lns = [slice(gi * hw, (gi + 1) * hw) for _, gi in ids]
    load = lambda ref: [ref[sl, ln] for sl, ln in zip(sls, lns)]
    par = lambda ref: [ref[:, ln] for ln in lns]
    r, k, v, wl, a_s = load(r_ref), load(k_ref), load(v_ref), load(wl_ref), load(a_ref)
    cum = each(lambda x: jnp.dot(tri_n, x, precision=HIGHEST, preferred_element_type=F32), wl)
    kk = each(lambda x, w: x * w, k, par(kk_ref))
    ksq = each(lambda x: head_sum(x * x), kk)
    kk = each(lambda x, s: x * lax.rsqrt(jnp.maximum(s, 1e-24)), kk, ksq)
    ke = each(lambda x, a, w: x * (1.0 + (a - 1.0) * w), k, a_s, par(ka_ref))
    cum_last = each(lambda x: x[n - 1:n, :], cum)
    e_neg = each(lambda x: jnp.exp(-x), cum)
    e_end = each(lambda x, xl: jnp.exp(xl - x), cum, cum_last)
    kka = each(lambda x, a: x * a, kk, a_s)
    lhs = each(lambda x, cu, w, rr: jnp.concatenate([to_blk(-x * jnp.exp(cu - w)), to_blk(rr * jnp.exp(cu))], axis=0),
               kk, cum, wl, r)
    rhs = each(lambda x, y, e: jnp.concatenate([to_blk(x * e), to_blk(y * e)], axis=0), kka, ke, e_neg)
    ends = each(lambda x, y, e: jnp.concatenate([to_blk(x * e), to_blk(y * e)], axis=0), kka, ke, e_end)
    vb = each(to_blk, v)
    cross = each(lambda a, b: lax.dot_general(a, b, NT_DIMS, preferred_element_type=F32), lhs, rhs)
    nab = each(lambda x: jnp.where(strict, x[:rows, :rows], 0.0), cross)
    aak = each(lambda x: jnp.where(strict, x[:rows, rows:], 0.0).astype(BF16), cross)
    rbk = each(lambda x: jnp.concatenate([jnp.where(incl, x[rows:, :rows], 0.0),
                                          jnp.where(incl, x[rows:, rows:], 0.0)], axis=1).astype(BF16), cross)
    inv = each(lambda x: eye + x, nab)
    pw = each(lambda x: x.astype(BF16), nab)
    for _ in range(max(1, int(math.ceil(math.log2(n))) - 1)):
        pw = each(lambda x: bdot(x, x).astype(BF16), pw)
        inv = each(lambda x, p: x + bdot(x.astype(BF16), p), inv, pw)
    inv = each(lambda x: x.astype(BF16), inv)
    akv = each(bdot, aak, vb)
    bonus = each(lambda rr, x, w, vv: head_sum(rr * x * w) * vv, r, ke, par(rk_ref), v)
    decay_end = each(jnp.exp, cum_last)
    ys = [None] * len(ids)
    for c in range(n_chunks):
        idx = [i for i, (cc, _) in enumerate(ids) if cc == c]
        take = lambda col: [col[i] for i in idx]
        s0 = [st_sc[ids[i][1]] for i in idx]
        ls = each(lambda a, s: lax.dot_general(a, s.astype(BF16), NT_DIMS, preferred_element_type=F32), take(lhs), s0)
        u = each(lambda m, x, y: bdot(m, (x[:rows] + y).astype(BF16)), take(inv), ls, take(akv))
        uv = each(lambda x, y: jnp.concatenate([x.astype(BF16), y], axis=0), u, take(vb))
        y = each(lambda x, m, z: from_blk(x[rows:] + bdot(m, z)), ls, take(rbk), uv)
        new = each(lambda s, d, z, e: s * d + lax.dot_general(z, e, TN_DIMS, preferred_element_type=F32),
                   s0, take(decay_end), uv, take(ends))
        for i, st, yy in zip(idx, new, y):
            st_sc[ids[i][1]] = st
            ys[i] = yy
    mu = each(lambda x: head_sum(x) * (1.0 / D_HEAD), ys)
    yc = each(lambda x, m: x - m, ys, mu)
    var = each(lambda x: head_sum(x * x) * (1.0 / D_HEAD), yc)
    yn = each(lambda x, s, w, b: x * lax.rsqrt(s + gn_eps) * w + b, yc, var, par(gw_ref), par(gb_ref))
    for sl, ln, x, b in zip(sls, lns, yn, bonus):
        o_ref[sl, ln] = ((x + b) * g_ref[sl, ln]).astype(o_ref.dtype)


def rwkv_scan(r, k, v, wl, a, g, k_k, k_a, r_k, gn_w, gn_b, batch, seq):
    t, width = r.shape
    hp = RWKV_HEADS_PER_STEP
    hw = hp * D_HEAD
    gw = _pick(width, (RWKV_GROUPS_PER_STEP * hw, hw))
    ch = RWKV_CHUNK
    tb = _pick(seq, (2 * ch, ch))
    nc = seq // tb
    xspec = pl.BlockSpec((tb, gw), lambda b, h, c: (b * nc + c, h))
    pspec = pl.BlockSpec((1, gw), lambda b, h, c: (0, h))
    vec = lambda p: p.reshape(1, width).astype(F32)
    return pl.pallas_call(
        functools.partial(_rwkv_scan_kernel, heads=hp, chunk=ch, gn_eps=D_GN_EPS),
        grid=(batch, width // gw, nc),
        in_specs=[xspec] * 6 + [pspec] * 5,
        out_specs=xspec,
        out_shape=jax.ShapeDtypeStruct((t, width), BF16),
        scratch_shapes=[pltpu.VMEM((gw // hw, hw, hw), F32)],
        compiler_params=_params("parallel", "parallel", "arbitrary"),
        name="rwkv_scan",
    )(r, k, v, wl, a, g, vec(k_k), vec(k_a), vec(r_k), vec(gn_w), vec(gn_b))


def _extract_top(work, dst_ref, count):
    for r in range(count):
        m = jnp.max(work, axis=0, keepdims=True)
        dst_ref[r:r + 1, :] = m
        work = jnp.where(work >= m, -jnp.inf, work)


def _peer_topk_kernel(q_ref, keys_ref, thr_ref, f1_ref, s2_ref, e2_ref, a1_sc, a2_sc, cand_sc, top_sc):
    kk = P_TOPK
    for h in range(P_HEADS):
        sc = []
        for half in range(2):
            qh = q_ref[:, (2 * h + half) * LANES:(2 * h + half + 1) * LANES]
            sc.append(lax.dot_general(keys_ref[h, half], qh, NT_DIMS, precision=HIGHEST,
                                      preferred_element_type=F32))
        _extract_top(sc[0], a1_sc, kk)
        _extract_top(sc[1], a2_sc, kk)
        cand_sc[...] = jnp.full(cand_sc.shape, -jnp.inf, F32)
        off = 0
        for i in range(kk):
            nj = kk // (i + 1)
            cand_sc[off:off + nj, :] = a1_sc[i:i + 1, :] + a2_sc[0:nj, :]
            off += nj
        _extract_top(cand_sc[...], top_sc, kk)
        top = top_sc[...]
        best = top[0:1, :]
        tau = top[kk - 1:kk, :]
        zsum = jnp.sum(jnp.exp(top - best), axis=0, keepdims=True)
        thr = jnp.full(sc[0].shape, jnp.inf, F32)
        for j in range(kk):
            bj = a2_sc[j:j + 1, :]
            thr = jnp.where(sc[0] + bj >= tau, bj, thr)
        thr_ref[h] = thr
        s2_ref[h] = sc[1]
        f1_ref[h] = jnp.exp(sc[0] - a1_sc[0:1, :]) / zsum
        e2_ref[h] = jnp.exp(sc[1] - a2_sc[0:1, :])


def peer_topk(q, keys):
    t = q.shape[0]
    tt = _pick(t, (256, 128))
    n_cand = sum(P_TOPK // (i + 1) for i in range(P_TOPK))
    n_cand = -(-n_cand // 8) * 8
    big = pl.BlockSpec((P_HEADS, P_NKEYS, tt), lambda i: (0, 0, i))
    big_shape = jax.ShapeDtypeStruct((P_HEADS, P_NKEYS, t), F32)
    return pl.pallas_call(
        _peer_topk_kernel,
        grid=(t // tt,),
        in_specs=[pl.BlockSpec((tt, q.shape[1]), lambda i: (i, 0)),
                  pl.BlockSpec(keys.shape, lambda i: (0, 0, 0, 0))],
        out_specs=[big, big, big, big],
        out_shape=[big_shape] * 4,
        scratch_shapes=[pltpu.VMEM((P_TOPK, tt), F32), pltpu.VMEM((P_TOPK, tt), F32),
                        pltpu.VMEM((n_cand, tt), F32), pltpu.VMEM((P_TOPK, tt), F32)],
        compiler_params=_params("parallel"),
        name="peer_topk",
    )(q, keys)


def _peer_dense_kernel(sc_ref, x_ref, u_ref, v_ref, thr_ref, f1_ref, s2_ref, e2_ref, o_ref, *, rows, chains):
    @pl.when(pl.program_id(1) == 0)
    def _():
        o_ref[...] = jnp.zeros(o_ref.shape, F32)

    tc = x_ref.shape[0] // chains
    hr = rows // 2
    dh = o_ref.shape[1] // 2
    tok = lambda ch: slice(ch * tc, (ch + 1) * tc)

    def activation(ch, half):
        es = slice(half * hr * P_NKEYS, (half + 1) * hr * P_NKEYS)
        return lax.dot_general(u_ref[es, :], x_ref[tok(ch), :], NT_DIMS, preferred_element_type=F32)

    def gated(ch, half, act):
        ts = tok(ch)
        act = act * sc_ref[0]
        act = 0.5 * act * (1.0 + lax.erf(act * (2.0 ** -0.5)))
        parts = []
        for r in range(hr):
            row = half * hr + r
            coef = None
            for h in range(P_HEADS):
                sel = s2_ref[h, :, ts] >= thr_ref[h, row:row + 1, ts]
                c = jnp.where(sel, e2_ref[h, :, ts] * f1_ref[h, row:row + 1, ts], 0.0)
                coef = c if coef is None else coef + c
            parts.append((coef * act[r * P_NKEYS:(r + 1) * P_NKEYS, :]).astype(BF16))
        return parts

    def project(ch, half, w):
        cs = slice(half * dh, (half + 1) * dh)
        o_ref[tok(ch), cs] += lax.dot_general(w, v_ref[:, cs], TN_DIMS, preferred_element_type=F32)

    acts = [activation(0, 0), activation(0, 1)]
    w_prev = None
    for ch in range(chains):
        nxt, parts = [], []
        for half in range(2):
            parts += gated(ch, half, acts[half])
            if ch + 1 < chains:
                nxt.append(activation(ch + 1, half))
            if w_prev is not None:
                project(ch - 1, half, w_prev)
        w_prev = jnp.concatenate(parts, axis=0)
        acts = nxt
    project(chains - 1, 0, w_prev)
    project(chains - 1, 1, w_prev)


def peer_dense(x, u, v, act_scale, thr, f1, s2, e2):
    t, d = x.shape
    n = u.shape[0]
    tt = _pick(t, (1024, 512, 256, 128))
    chains = 2 if tt % 256 == 0 else 1
    rows = 4
    tn = rows * P_NKEYS
    thr_r = thr.reshape(P_HEADS, P_NKEYS // rows, rows, t)
    f1r = f1.reshape(P_HEADS, P_NKEYS // rows, rows, t)
    once = pl.Buffered(1)
    rowspec = pl.BlockSpec((P_HEADS, None, rows, tt), lambda i, e: (0, e, 0, i))
    fullspec = pl.BlockSpec((P_HEADS, P_NKEYS, tt), lambda i, e: (0, 0, i), pipeline_mode=once)
    return pl.pallas_call(
        functools.partial(_peer_dense_kernel, rows=rows, chains=chains),
        grid=(t // tt, n // tn),
        in_specs=[pl.BlockSpec(memory_space=pltpu.SMEM),
                  pl.BlockSpec((tt, d), lambda i, e: (i, 0), pipeline_mode=once),
                  pl.BlockSpec((tn, d), lambda i, e: (e, 0)),
                  pl.BlockSpec((tn, d), lambda i, e: (e, 0)),
                  rowspec, rowspec, fullspec, fullspec],
        out_specs=pl.BlockSpec((tt, d), lambda i, e: (i, 0), pipeline_mode=once),
        out_shape=jax.ShapeDtypeStruct((t, d), F32),
        compiler_params=_params("parallel", "arbitrary"),
        name="peer_dense",
    )(act_scale.astype(F32), x, u, v, thr_r, f1r, s2, e2)


def _fp8_scaled(a):
    amax = jnp.max(jnp.abs(a)).astype(F32)
    k = jnp.floor(jnp.log2(FP8_TARGET_MAX / jnp.maximum(amax, 1e-30)))
    scale = jnp.exp2(k)
    return (a.astype(F32) * scale).astype(FP8), 1.0 / scale


def peer_ffn(hn, w_q, keys, u, v):
    q = matmul([hn], [w_q.astype(BF16)], F32)
    thr, f1, s2, e2 = peer_topk(q, keys)
    x8, inv_x = _fp8_scaled(hn)
    u8, inv_u = _fp8_scaled(u)
    return peer_dense(x8, u8, v.astype(BF16), (inv_x * inv_u).reshape(1), thr, f1, s2, e2)


def even_mixer(hn, w_in, w_out, lam_params, subln_gain, lam_init, w_gate, b_gate, gla_norm, batch, seq):
    na = w_out.shape[0] // 2
    a_heads = na // A_V_DIM
    b_heads = na // B_V_DIM
    sizes = (2 * a_heads * A_QK_DIM, 2 * a_heads * A_QK_DIM, na, b_heads * B_K_DIM, b_heads * B_K_DIM, na,
             w_gate.shape[0], na)
    names = ("aq", "ak", "av", "bq", "bk", "bv", "lo", "og")
    start, off = {}, 0
    for nm, sz in zip(names, sizes):
        start[nm] = off
        off += sz
    w_main = jnp.concatenate([w_in[:, :start["lo"]], w_in[:, start["og"]:]], axis=1).astype(BF16)
    cols = dict(start)
    cols["og"] = start["lo"]
    w_lo = _pad_cols(w_in[:, start["lo"]:start["og"]], LANES).astype(BF16)
    p = matmul([hn], [w_main], BF16)
    p_lo = matmul([hn], [w_lo], F32)
    qk = rope(p, rope_tables(seq, A_ROT), seq, width=start["av"], col_block=0, half=A_ROT // 2)
    lp = lam_params.astype(F32)
    lam = jnp.exp(jnp.sum(lp[0] * lp[1])) - jnp.exp(jnp.sum(lp[2] * lp[3])) + lam_init
    o_a = diff_attention(qk, p, lam, subln_gain, lam_init, batch, seq, start["av"])
    w_gate_p = jnp.pad(w_gate.astype(F32), ((0, LANES - w_gate.shape[0]), (0, 0)))
    o_b = gla(p, p_lo, w_gate_p, b_gate.astype(F32), gla_norm.astype(F32), batch, seq, cols)
    w_out = w_out.astype(BF16)
    return matmul([o_a, o_b], [w_out[:na], w_out[na:]], F32)


def odd_mixer(hn, w_in, w_out, q_norm, kv_norm, w_uq, w_ukv, mu, w0, w2, a0, a2, g2,
              k_k, k_a, r_k, gn_w, gn_b, batch, seq):
    q_rank, kv_rank = q_norm.shape[0], kv_norm.shape[0]
    width = w0.shape[0]
    rw, ra, rg = w2.shape[0], a2.shape[0], g2.shape[0]
    c_cols = q_rank + kv_rank + C_ROPE
    heads = w_uq.shape[1] // (C_NOPE + C_ROPE)
    c_pad = -(-(q_rank + kv_rank + LANES) // 256) * 256
    w_c = _pad_cols(w_in[:, :c_cols], c_pad).astype(BF16)
    pad_l = lambda m: -(-m // LANES) * LANES
    d0 = c_cols
    segs, mus, off = [], [], d0
    for sz in (width, width, width, rw, ra, rg):
        segs.append(_pad_cols(w_in[:, off:off + sz], pad_l(sz)))
        mus.append(jnp.pad(mu[off - d0:off - d0 + sz], (0, pad_l(sz) - sz)))
        off += sz
    w_d = jnp.concatenate(segs, axis=1).astype(BF16)
    mu_d = jnp.concatenate(mus).astype(F32)
    pad_r = lambda w: jnp.pad(w.astype(F32), ((0, pad_l(w.shape[0]) - w.shape[0]), (0, 0)))

    p_c = matmul([hn], [w_c], F32)
    p_d = matmul([hn], [w_d], F32)

    cq = add_rmsnorm([p_c], q_norm, BF16, width=q_rank, col_block=0)
    ckv = add_rmsnorm([p_c], kv_norm, BF16, width=kv_rank, col_block=q_rank // kv_rank)
    wq3 = w_uq.reshape(q_rank, heads, C_NOPE + C_ROPE)
    wq_nope = wq3[:, :, :C_NOPE].reshape(q_rank, heads * C_NOPE)
    wq_rope = jnp.pad(wq3[:, :, C_NOPE:], ((0, 0), (0, 0), (0, LANES - C_ROPE))).reshape(q_rank, heads * LANES)
    q_all = matmul([cq], [jnp.concatenate([wq_nope, wq_rope], axis=1).astype(BF16)], BF16)
    wkv3 = w_ukv.reshape(kv_rank, heads, C_NOPE + C_V)
    w_kv = jnp.concatenate([wkv3[:, :, :C_NOPE].reshape(kv_rank, heads * C_NOPE),
                            wkv3[:, :, C_NOPE:].reshape(kv_rank, heads * C_V)], axis=1).astype(BF16)
    kv = matmul([ckv], [w_kv], BF16)
    tables = rope_tables(seq, C_ROPE)
    q_rope = rope(q_all, tables, seq, width=heads * LANES, col_block=1, half=C_ROPE // 2)
    k_rope = rope(p_c, tables, seq, width=LANES, col_block=(q_rank + kv_rank) // LANES, half=C_ROPE // 2)
    o_c = mla_attention(q_all, q_rope, kv, k_rope, batch, seq)

    r, k, v, wl, a, g = rwkv_prep(p_d, mu_d, w0.astype(F32), pad_r(w2), a0.astype(F32), pad_r(a2), pad_r(g2),
                                  batch, seq)
    o_d = rwkv_scan(r, k, v, wl, a, g, k_k, k_a, r_k, gn_w, gn_b, batch, seq)
    nc = o_c.shape[1]
    w_out = w_out.astype(BF16)
    return matmul([o_c, o_d], [w_out[:nc], w_out[nc:]], F32)


def kernel(x, norm_mix, norm_ffn, norm_final, even_w_in, even_w_out, diff_lambda, diff_subln, gla_w_gate, gla_b_gate, gla_norm, odd_w_in, odd_w_out, mla_q_norm, mla_kv_norm, mla_w_uq, mla_w_ukv, rwkv_mu, rwkv_w0, rwkv_w2, rwkv_a0, rwkv_a2, rwkv_g2, rwkv_k_k, rwkv_k_a, rwkv_r_k, rwkv_gn_w, rwkv_gn_b, peer_w_q, peer_keys, peer_u, peer_v):
    batch, seq, d = x.shape
    depth = norm_mix.shape[0]
    h = x.reshape(batch * seq, d)
    pending = []
    for layer in range(depth):
        j = layer // 2
        if pending:
            h, hn = add_rmsnorm([h] + pending, norm_mix[layer], BF16, emit_sum=True)
        else:
            hn = add_rmsnorm([h], norm_mix[layer], BF16)
        if layer % 2 == 0:
            lam_init = 0.8 - 0.6 * math.exp(-0.3 * layer)
            mix = even_mixer(hn, even_w_in[j], even_w_out[j], diff_lambda[j], diff_subln[j], lam_init,
                             gla_w_gate[j], gla_b_gate[j], gla_norm[j], batch, seq)
        else:
            mix = odd_mixer(hn, odd_w_in[j], odd_w_out[j], mla_q_norm[j], mla_kv_norm[j], mla_w_uq[j],
                            mla_w_ukv[j], rwkv_mu[j], rwkv_w0[j], rwkv_w2[j], rwkv_a0[j], rwkv_a2[j],
                            rwkv_g2[j], rwkv_k_k[j], rwkv_k_a[j], rwkv_r_k[j], rwkv_gn_w[j], rwkv_gn_b[j],
                            batch, seq)
        h, hn2 = add_rmsnorm([h, mix], norm_ffn[layer], BF16, emit_sum=True)
        pending = [peer_ffn(hn2, peer_w_q[layer], peer_keys[layer], peer_u[layer], peer_v[layer])]
    out = add_rmsnorm([h] + pending, norm_final, F32)
    return out.reshape(batch, seq, d).astype(x.dtype)
```

```python
import functools
import math

import jax
import jax.numpy as jnp
from jax import lax
from jax.experimental import pallas as pl
from jax.experimental.pallas import tpu as pltpu

F32 = jnp.float32
BF16 = jnp.bfloat16
FP8 = jnp.float8_e4m3fn
FP8_TARGET_MAX = 256.0
HIGHEST = lax.Precision.HIGHEST

LANES = 128
BF16_ROWS = 16
VMEM_LIMIT = 56 * 1024 * 1024

EPS = 1e-6
ROPE_THETA = 500000.0
A_QK_DIM = 128
A_V_DIM = 256
A_ROT = A_QK_DIM // 4
B_V_DIM = 512
B_K_DIM = 256
B_GATE_TAU = 16.0
GLA_CHUNK = 64
C_NOPE = 128
C_ROPE = 64
C_V = 128
D_HEAD = 64
D_GN_EPS = 64e-5
RWKV_CHUNK = 64
ATTN_BLOCKS = (1024, 512, 256, 128)
ATTN_SUB = 4
RWKV_HEADS_PER_STEP = 4
RWKV_GROUPS_PER_STEP = 4
P_HEADS = 8
P_NKEYS = 128
P_TOPK = 16

NT_DIMS = (((1,), (1,)), ((), ()))
TN_DIMS = (((0,), (0,)), ((), ()))


def _params(*sem):
    return pltpu.CompilerParams(dimension_semantics=sem, vmem_limit_bytes=VMEM_LIMIT)


def _pick(n, cands):
    for c in cands:
        if n % c == 0:
            return c
    raise ValueError(f"no tile in {cands} divides {n}")


def _pad_cols(w, n):
    return jnp.pad(w, ((0, 0), (0, n - w.shape[1])))


def _norm_kernel(*refs, eps, n_add, emit_sum):
    xs = refs[:n_add]
    g_ref = refs[n_add]
    outs = refs[n_add + 1:]
    x = xs[0][...].astype(F32)
    for r in xs[1:]:
        x = x + r[...].astype(F32)
    y = x * lax.rsqrt(jnp.mean(x * x, axis=-1, keepdims=True) + eps) * g_ref[...]
    if emit_sum:
        outs[0][...] = x
        outs[1][...] = y.astype(outs[1].dtype)
    else:
        outs[0][...] = y.astype(outs[0].dtype)


def add_rmsnorm(xs, g, out_dtype, *, emit_sum=False, width=None, col_block=0, eps=EPS):
    t = xs[0].shape[0]
    width = width or xs[0].shape[1]
    tr = _pick(t, (128, 64, 32, 16, 8))
    spec = pl.BlockSpec((tr, width), lambda i: (i, col_block))
    ospec = pl.BlockSpec((tr, width), lambda i: (i, 0))
    out_shape = [jax.ShapeDtypeStruct((t, width), out_dtype)]
    out_specs = [ospec]
    if emit_sum:
        out_shape = [jax.ShapeDtypeStruct((t, width), F32)] + out_shape
        out_specs = [ospec, ospec]
    res = pl.pallas_call(
        functools.partial(_norm_kernel, eps=eps, n_add=len(xs), emit_sum=emit_sum),
        grid=(t // tr,),
        in_specs=[spec] * len(xs) + [pl.BlockSpec((1, width), lambda i: (0, 0))],
        out_specs=out_specs,
        out_shape=out_shape,
        compiler_params=_params("parallel"),
        name="add_rmsnorm",
    )(*xs, g.reshape(1, width).astype(F32))
    return res if emit_sum else res[0]


def _mm_kernel(*refs, n_in, has_residual):
    o_ref = refs[-1]
    acc = jnp.dot(refs[0][...], refs[n_in][...], preferred_element_type=F32)
    for i in range(1, n_in):
        acc = acc + jnp.dot(refs[i][...], refs[n_in + i][...], preferred_element_type=F32)
    if has_residual:
        acc = acc + refs[2 * n_in][...]
    o_ref[...] = acc.astype(o_ref.dtype)


def matmul(xs, ws, out_dtype, residual=None):
    m = xs[0].shape[0]
    n = ws[0].shape[1]
    tm = _pick(m, (1024, 512, 256, 128, 64, 32, 16))
    tn = _pick(n, (512, 768, 640, 256, 128))
    in_specs = [pl.BlockSpec((tm, x.shape[1]), lambda i, j: (i, 0)) for x in xs]
    in_specs += [pl.BlockSpec((w.shape[0], tn), lambda i, j: (0, j)) for w in ws]
    extra = []
    if residual is not None:
        in_specs.append(pl.BlockSpec((tm, tn), lambda i, j: (i, j)))
        extra = [residual]
    return pl.pallas_call(
        functools.partial(_mm_kernel, n_in=len(xs), has_residual=residual is not None),
        grid=(m // tm, n // tn),
        in_specs=in_specs,
        out_specs=pl.BlockSpec((tm, tn), lambda i, j: (i, j)),
        out_shape=jax.ShapeDtypeStruct((m, n), out_dtype),
        compiler_params=_params("parallel", "arbitrary"),
        name="matmul",
    )(*xs, *ws, *extra)


def rope_tables(seq, rot_dim):
    half = rot_dim // 2
    inv = 1.0 / (ROPE_THETA ** (jnp.arange(0, rot_dim, 2, dtype=F32) / rot_dim))
    ang = jnp.arange(seq, dtype=F32)[:, None] * inv[None, :]
    cos, sin = jnp.cos(ang), jnp.sin(ang)
    z = jnp.zeros((seq, LANES - 2 * half), F32)
    zh = jnp.zeros((seq, half), F32)
    c = jnp.concatenate([cos, cos, jnp.ones_like(z)], axis=1)
    s_up = jnp.concatenate([zh, sin, z], axis=1)
    s_dn = jnp.concatenate([-sin, zh, z], axis=1)
    return c, s_up, s_dn


def _rope_kernel(x_ref, c_ref, su_ref, sd_ref, o_ref, *, half, groups):
    c, su, sd = c_ref[...], su_ref[...], sd_ref[...]
    for g in range(groups):
        sl = slice(g * LANES, (g + 1) * LANES)
        x = x_ref[:, sl].astype(F32)
        y = x * c + pltpu.roll(x, half, 1) * su + pltpu.roll(x, LANES - half, 1) * sd
        o_ref[:, sl] = y.astype(o_ref.dtype)


def rope(x, tables, seq, *, width, col_block, half):
    t = x.shape[0]
    tr = _pick(seq, (256, 128, 64, 32, 16))
    nb = seq // tr
    tspec = pl.BlockSpec((tr, LANES), lambda i: (i % nb, 0))
    return pl.pallas_call(
        functools.partial(_rope_kernel, half=half, groups=width // LANES),
        grid=(t // tr,),
        in_specs=[pl.BlockSpec((tr, width), lambda i: (i, col_block)), tspec, tspec, tspec],
        out_specs=pl.BlockSpec((tr, width), lambda i: (i, 0)),
        out_shape=jax.ShapeDtypeStruct((t, width), BF16),
        compiler_params=_params("parallel"),
        name="rope",
    )(x, *tables)


LOG2E = 1.4426950408889634


def _scaled_q(q, scale):
    return (q.astype(F32) * (scale * LOG2E)).astype(BF16)


def _softmax_chains(chains):
    s = [lax.dot_general(q, k, NT_DIMS, preferred_element_type=F32) for q, k, *_ in chains]
    def masked(x, tail):
        if tail is None:
            return x
        w = tail.shape[1]
        last = jnp.where(tail, x[:, x.shape[1] - w:], -jnp.inf)
        return last if w == x.shape[1] else jnp.concatenate([x[:, :x.shape[1] - w], last], axis=1)

    s = [masked(x, c[6]) for x, c in zip(s, chains)]
    m_prev = [c[3][...] for c in chains]
    m_new = [jnp.maximum(mp, jnp.max(x, axis=-1, keepdims=True)) for mp, x in zip(m_prev, s)]
    alpha = [jnp.exp2(mp - mn) for mp, mn in zip(m_prev, m_new)]
    p = [jnp.exp2(x - mn) for x, mn in zip(s, m_new)]
    pv = [jnp.dot(x.astype(c[2].dtype), c[2], preferred_element_type=F32) for x, c in zip(p, chains)]
    for c, a, x, y, mn in zip(chains, alpha, p, pv, m_new):
        c[4][...] = a * c[4][...] + jnp.sum(x, axis=-1, keepdims=True)
        c[5][...] = a * c[5][...] + y
        c[3][...] = mn


def _causal_sweep(i, tq, step):
    def body(j, carry):
        step(pl.ds(pl.multiple_of(j * tq, tq), tq), False)
        return carry
    lax.fori_loop(0, i, body, 0)
    step(pl.ds(pl.multiple_of(i * tq, tq), tq), True)


def _row_groups(q, k, v, m_sc, l_sc, acc_sc, diagonal):
    rs = q.shape[0] // ATTN_SUB
    chains = []
    tri = _tril(rs)
    for g in range(ATTN_SUB):
        qg = q[g * rs:(g + 1) * rs]
        if diagonal:
            r = g * rs + lax.broadcasted_iota(jnp.int32, (rs, k.shape[0]), 0)
            c = lax.broadcasted_iota(jnp.int32, (rs, k.shape[0]), 1)
            chains.append((qg, k, v, m_sc.at[g], l_sc.at[g], acc_sc.at[g], c <= r))
        else:
            chains.append((qg, k, v, m_sc.at[g], l_sc.at[g], acc_sc.at[g], None))
    return chains


def _diff_attn_kernel(lam_ref, q_ref, k_ref, v_ref, g_ref, o_ref, m_sc, l_sc, acc_sc, *, scale, tq, out_scale, eps):
    i = pl.program_id(2)
    m_sc[...] = jnp.full(m_sc.shape, -jnp.inf, F32)
    l_sc[...] = jnp.zeros(l_sc.shape, F32)
    acc_sc[...] = jnp.zeros(acc_sc.shape, F32)
    qs = [_scaled_q(q_ref[:, mp * A_QK_DIM:(mp + 1) * A_QK_DIM], scale) for mp in range(2)]

    def step(rows, diagonal):
        v = v_ref[rows, :]
        chains = []
        for mp in range(2):
            k = k_ref[rows, mp * A_QK_DIM:(mp + 1) * A_QK_DIM]
            chains += _row_groups(qs[mp], k, v, m_sc.at[mp], l_sc.at[mp], acc_sc.at[mp], diagonal)
        _softmax_chains(chains)

    _causal_sweep(i, tq, step)
    rs = tq // ATTN_SUB
    for g in range(ATTN_SUB):
        o = acc_sc[0, g] / l_sc[0, g] - lam_ref[0] * (acc_sc[1, g] / l_sc[1, g])
        o = o * lax.rsqrt(jnp.mean(o * o, axis=-1, keepdims=True) + eps) * g_ref[...] * out_scale
        o_ref[g * rs:(g + 1) * rs, :] = o.astype(o_ref.dtype)


def diff_attention(qk, p, lam, subln_gain, lam_init, batch, seq, v_col):
    t = qk.shape[0]
    heads = qk.shape[1] // (4 * A_QK_DIM)
    hw = 2 * A_QK_DIM
    tq = _pick(seq, ATTN_BLOCKS)
    rs = tq // ATTN_SUB
    nq = seq // tq
    vb = v_col // A_V_DIM
    kern = functools.partial(_diff_attn_kernel, scale=A_QK_DIM ** -0.5, tq=tq, out_scale=1.0 - lam_init, eps=EPS)
    return pl.pallas_call(
        kern,
        grid=(batch, heads, nq),
        in_specs=[
            pl.BlockSpec(memory_space=pltpu.SMEM),
            pl.BlockSpec((tq, hw), lambda b, h, i: (b * nq + i, h)),
            pl.BlockSpec((seq, hw), lambda b, h, i: (b, heads + h)),
            pl.BlockSpec((seq, A_V_DIM), lambda b, h, i: (b, vb + h)),
            pl.BlockSpec((1, A_V_DIM), lambda b, h, i: (0, 0)),
        ],
        out_specs=pl.BlockSpec((tq, A_V_DIM), lambda b, h, i: (b * nq + i, h)),
        out_shape=jax.ShapeDtypeStruct((t, heads * A_V_DIM), BF16),
        scratch_shapes=[pltpu.VMEM((2, ATTN_SUB, rs, 1), F32), pltpu.VMEM((2, ATTN_SUB, rs, 1), F32),
                        pltpu.VMEM((2, ATTN_SUB, rs, A_V_DIM), F32)],
        compiler_params=_params("parallel", "parallel", "arbitrary"),
        name="diff_attention",
    )(lam.reshape(1).astype(F32), qk, qk, p, subln_gain.reshape(1, A_V_DIM).astype(F32))


def _mla_attn_kernel(qn_ref, qr_ref, kn_ref, kr_ref, v_ref, o_ref, m_sc, l_sc, acc_sc, *, scale, tq):
    i = pl.program_id(2)
    m_sc[...] = jnp.full(m_sc.shape, -jnp.inf, F32)
    l_sc[...] = jnp.zeros(l_sc.shape, F32)
    acc_sc[...] = jnp.zeros(acc_sc.shape, F32)
    q = _scaled_q(jnp.concatenate([qn_ref[...], qr_ref[...]], axis=-1), scale)

    def step(rows, diagonal):
        k = jnp.concatenate([kn_ref[rows, :], kr_ref[rows, :]], axis=-1)
        _softmax_chains(_row_groups(q, k, v_ref[rows, :], m_sc, l_sc, acc_sc, diagonal))

    _causal_sweep(i, tq, step)
    rs = tq // ATTN_SUB
    for g in range(ATTN_SUB):
        o_ref[g * rs:(g + 1) * rs, :] = (acc_sc[g] / l_sc[g]).astype(o_ref.dtype)


def mla_attention(q_all, q_rope, kv, k_rope, batch, seq):
    t = kv.shape[0]
    heads = kv.shape[1] // (C_NOPE + C_V)
    tq = _pick(seq, ATTN_BLOCKS)
    rs = tq // ATTN_SUB
    nq = seq // tq
    qmap = lambda b, h, i: (b * nq + i, h)
    return pl.pallas_call(
        functools.partial(_mla_attn_kernel, scale=(C_NOPE + C_ROPE) ** -0.5, tq=tq),
        grid=(batch, heads, nq),
        in_specs=[
            pl.BlockSpec((tq, C_NOPE), qmap),
            pl.BlockSpec((tq, LANES), qmap),
            pl.BlockSpec((seq, C_NOPE), lambda b, h, i: (b, h)),
            pl.BlockSpec((seq, LANES), lambda b, h, i: (b, 0)),
            pl.BlockSpec((seq, C_V), lambda b, h, i: (b, heads + h)),
        ],
        out_specs=pl.BlockSpec((tq, C_V), qmap),
        out_shape=jax.ShapeDtypeStruct((t, heads * C_V), BF16),
        scratch_shapes=[pltpu.VMEM((ATTN_SUB, rs, 1), F32), pltpu.VMEM((ATTN_SUB, rs, 1), F32),
                        pltpu.VMEM((ATTN_SUB, rs, C_V), F32)],
        compiler_params=_params("parallel", "parallel", "arbitrary"),
        name="mla_attention",
    )(q_all, q_rope, kv, k_rope, kv)


def _tril(n, strict=False):
    r = lax.broadcasted_iota(jnp.int32, (n, n), 0)
    c = lax.broadcasted_iota(jnp.int32, (n, n), 1)
    return (r > c) if strict else (r >= c)


def _gla_kernel(q_ref, k_ref, v_ref, og_ref, lo_ref, wg_ref, bg_ref, gn_ref, o_ref, st_sc, *, scale, eps, chunk, heads):
    @pl.when(pl.program_id(1) == 0)
    def _():
        st_sc[...] = jnp.zeros(st_sc.shape, F32)

    n = chunk
    n_chunks = q_ref.shape[0] // n
    dk, dv = B_K_DIM, B_V_DIM
    each = lambda f, *cols: [f(*a) for a in zip(*cols)]
    tri = _tril(n)
    trif = tri.astype(F32)
    rows = [slice(c * n, (c + 1) * n) for c in range(n_chunks)]
    z = [jnp.dot(lo_ref[r, :], wg_ref[...], precision=HIGHEST, preferred_element_type=F32) + bg_ref[...] for r in rows]
    log_a = each(lambda x: (jnp.minimum(x, 0.0) - jnp.log1p(jnp.exp(-jnp.abs(x)))) / B_GATE_TAU, z)
    b_all = each(lambda x: jnp.dot(trif, x, precision=HIGHEST, preferred_element_type=F32), log_a)
    ids = [(c, h) for c in range(n_chunks) for h in range(heads)]
    kl = [slice(h * dk, (h + 1) * dk) for _, h in ids]
    vl = [slice(h * dv, (h + 1) * dv) for _, h in ids]
    rw = [rows[c] for c, _ in ids]
    b = [b_all[c][:, s] for (c, _), s in zip(ids, kl)]
    b_last = each(lambda x: x[n - 1:n, :], b)
    q = [q_ref[r, s].astype(F32) * scale for r, s in zip(rw, kl)]
    k = [k_ref[r, s].astype(F32) for r, s in zip(rw, kl)]
    v = [v_ref[r, s] for r, s in zip(rw, vl)]
    qe = each(lambda x, bb: (x * jnp.exp(bb)).astype(BF16), q, b)
    ke = each(lambda x, bb: (x * jnp.exp(-bb)).astype(BF16), k, b)
    kend = each(lambda x, bb, bl: (x * jnp.exp(bl - bb)).astype(BF16), k, b, b_last)
    attn = each(lambda a, c: jnp.where(tri, lax.dot_general(a, c, NT_DIMS, preferred_element_type=F32), 0.0).astype(BF16),
                qe, ke)
    o_intra = each(lambda a, vv: jnp.dot(a, vv, preferred_element_type=F32), attn, v)
    decay = each(jnp.exp, b_last)
    outs = [None] * len(ids)
    for c in range(n_chunks):
        idx = [i for i, (cc, _) in enumerate(ids) if cc == c]
        st = [st_sc[ids[i][1]] for i in idx]
        o_inter = [lax.dot_general(qe[i], s.astype(BF16), NT_DIMS, preferred_element_type=F32) for i, s in zip(idx, st)]
        new = [s * decay[i] + lax.dot_general(v[i], kend[i], TN_DIMS, preferred_element_type=F32)
               for i, s in zip(idx, st)]
        for i, s_new, oi in zip(idx, new, o_inter):
            st_sc[ids[i][1]] = s_new
            outs[i] = oi + o_intra[i]
    for r, s, o in zip(rw, vl, outs):
        o = o * lax.rsqrt(jnp.mean(o * o, axis=-1, keepdims=True) + eps) * gn_ref[...]
        og = og_ref[r, s].astype(F32)
        o_ref[r, s] = (o * (og * jax.nn.sigmoid(og))).astype(o_ref.dtype)


def gla(p, p_lo, w_gate, b_gate, gn, batch, seq, cols):
    t = p.shape[0]
    heads = w_gate.shape[1] // B_K_DIM
    ch = GLA_CHUNK
    tb = _pick(seq, (2 * ch, ch))
    nb = seq // tb
    kw, vw = heads * B_K_DIM, heads * B_V_DIM
    row = lambda b, c: b * nb + c
    return pl.pallas_call(
        functools.partial(_gla_kernel, scale=B_K_DIM ** -0.5, eps=EPS, chunk=ch, heads=heads),
        grid=(batch, nb),
        in_specs=[
            pl.BlockSpec((tb, kw), lambda b, c: (row(b, c), cols["bq"] // kw)),
            pl.BlockSpec((tb, kw), lambda b, c: (row(b, c), cols["bk"] // kw)),
            pl.BlockSpec((tb, vw), lambda b, c: (row(b, c), cols["bv"] // vw)),
            pl.BlockSpec((tb, vw), lambda b, c: (row(b, c), cols["og"] // vw)),
            pl.BlockSpec((tb, LANES), lambda b, c: (row(b, c), 0)),
            pl.BlockSpec((LANES, kw), lambda b, c: (0, 0)),
            pl.BlockSpec((1, kw), lambda b, c: (0, 0)),
            pl.BlockSpec((1, B_V_DIM), lambda b, c: (0, 0)),
        ],
        out_specs=pl.BlockSpec((tb, vw), lambda b, c: (row(b, c), 0)),
        out_shape=jax.ShapeDtypeStruct((t, vw), BF16),
        scratch_shapes=[pltpu.VMEM((heads, B_V_DIM, B_K_DIM), F32)],
        compiler_params=_params("parallel", "arbitrary"),
        name="gla",
    )(p, p, p, p, p_lo, w_gate, b_gate.reshape(1, -1), gn.reshape(1, -1))


def _rwkv_prep_kernel(pd_ref, mu_ref, w0_ref, w2_ref, a0_ref, a2_ref, g2_ref,
                      r_ref, k_ref, v_ref, wl_ref, a_ref, g_ref, carry_sc, *, width, rw, ra):
    @pl.when(pl.program_id(1) == 0)
    def _():
        carry_sc[...] = jnp.zeros(carry_sc.shape, F32)

    x = pd_ref[...]
    n = x.shape[0]
    first = lax.broadcasted_iota(jnp.int32, x.shape, 0) == 0
    prev = jnp.where(first, carry_sc[...], pltpu.roll(x, 1, 0))
    carry_sc[...] = x[n - 1:n, :]
    xs = x + (prev - x) * mu_ref[...]
    r_ref[...] = xs[:, 0:width]
    k_ref[...] = xs[:, width:2 * width]
    v_ref[...] = xs[:, 2 * width:3 * width]
    c0 = 3 * width
    xw, xa, xg = xs[:, c0:c0 + rw], xs[:, c0 + rw:c0 + rw + ra], xs[:, c0 + rw + ra:]
    zw = w0_ref[...] + jnp.dot(jnp.tanh(xw), w2_ref[...], precision=HIGHEST, preferred_element_type=F32)
    w = -(jnp.maximum(-zw, 0.0) + jnp.log1p(jnp.exp(-jnp.abs(zw)))) - 0.5
    wl_ref[...] = -jnp.exp(w)
    za = a0_ref[...] + jnp.dot(xa, a2_ref[...], precision=HIGHEST, preferred_element_type=F32)
    a_ref[...] = jax.nn.sigmoid(za)
    g_ref[...] = jnp.dot(jax.nn.sigmoid(xg), g2_ref[...], precision=HIGHEST, preferred_element_type=F32)


def rwkv_prep(pd, mu, w0, w2, a0, a2, g2, batch, seq):
    t, cols = pd.shape
    width = w0.shape[-1]
    rw, ra, rg = w2.shape[0], a2.shape[0], g2.shape[0]
    tb = _pick(seq, (256, 128, 64, 32, 16, 8))
    nb = seq // tb
    full = lambda shape: pl.BlockSpec(shape, lambda b, s: (0, 0))
    ospec = pl.BlockSpec((tb, width), lambda b, s: (b * nb + s, 0))
    return pl.pallas_call(
        functools.partial(_rwkv_prep_kernel, width=width, rw=rw, ra=ra),
        grid=(batch, nb),
        in_specs=[pl.BlockSpec((tb, cols), lambda b, s: (b * nb + s, 0)), full((1, cols)), full((1, width)),
                  full((rw, width)), full((1, width)), full((ra, width)), full((rg, width))],
        out_specs=[ospec] * 6,
        out_shape=[jax.ShapeDtypeStruct((t, width), F32)] * 6,
        scratch_shapes=[pltpu.VMEM((1, cols), F32)],
        compiler_params=_params("parallel", "arbitrary"),
        name="rwkv_prep",
    )(pd, mu.reshape(1, cols), w0.reshape(1, width), w2, a0.reshape(1, width), a2, g2)


def _rwkv_scan_kernel(r_ref, k_ref, v_ref, wl_ref, a_ref, g_ref, kk_ref, ka_ref, rk_ref, gw_ref, gb_ref,
                      o_ref, st_sc, *, heads, chunk, gn_eps):
    @pl.when(pl.program_id(2) == 0)
    def _():
        st_sc[...] = jnp.zeros(st_sc.shape, F32)

    n = chunk
    hw = heads * D_HEAD
    rows = heads * n
    ri = lax.broadcasted_iota(jnp.int32, (rows, hw), 0)
    ci = lax.broadcasted_iota(jnp.int32, (rows, hw), 1)
    same_head = (ri // n) == (ci // D_HEAD)
    rt_i = lax.broadcasted_iota(jnp.int32, (rows, rows), 0)
    ct_i = lax.broadcasted_iota(jnp.int32, (rows, rows), 1)
    strict = (rt_i % n) > (ct_i % n)
    incl = (rt_i % n) >= (ct_i % n)
    eye = (rt_i == ct_i).astype(F32)
    li = lax.broadcasted_iota(jnp.int32, (hw, hw), 0)
    lj = lax.broadcasted_iota(jnp.int32, (hw, hw), 1)
    head_ones = ((li // D_HEAD) == (lj // D_HEAD)).astype(BF16)
    tri_n = _tril(n).astype(F32)

    def to_blk(x):
        return jnp.where(same_head, jnp.concatenate([x] * heads, axis=0), 0.0).astype(BF16)

    def from_blk(y):
        out = y[0:n]
        for h in range(1, heads):
            out = out + y[h * n:(h + 1) * n]
        return out

    def head_sum(x):
        hi = x.astype(BF16)
        lo = (x - hi.astype(F32)).astype(BF16)
        s = jnp.dot(jnp.concatenate([hi, lo], axis=0), head_ones, preferred_element_type=F32)
        return s[:n] + s[n:]

    bdot = functools.partial(jnp.dot, preferred_element_type=F32)
    each = lambda f, *cols: [f(*a) for a in zip(*cols)]
    n_chunks, n_groups = r_ref.shape[0] // n, r_ref.shape[1] // hw
    ids = [(c, gi) for c in range(n_chunks) for gi in range(n_groups)]
    sls = [slice(c * n, (c + 1) * n) for c, _ in ids]
    lns = [slice(gi * hw, (gi + 1) * hw) for _, gi in ids]
    load = lambda ref: [ref[sl, ln] for sl, ln in zip(sls, lns)]
    par = lambda ref: [ref[:, ln] for ln in lns]
    r, k, v, wl, a_s = load(r_ref), load(k_ref), load(v_ref), load(wl_ref), load(a_ref)
    cum = each(lambda x: jnp.dot(tri_n, x, precision=HIGHEST, preferred_element_type=F32), wl)
    kk = each(lambda x, w: x * w, k, par(kk_ref))
    ksq = each(lambda x: head_sum(x * x), kk)
    kk = each(lambda x, s: x * lax.rsqrt(jnp.maximum(s, 1e-24)), kk, ksq)
    ke = each(lambda x, a, w: x * (1.0 + (a - 1.0) * w), k, a_s, par(ka_ref))
    cum_last = each(lambda x: x[n - 1:n, :], cum)
    e_neg = each(lambda x: jnp.exp(-x), cum)
    e_end = each(lambda x, xl: jnp.exp(xl - x), cum, cum_last)
    kka = each(lambda x, a: x * a, kk, a_s)
    lhs = each(lambda x, cu, w, rr: jnp.concatenate([to_blk(-x * jnp.exp(cu - w)), to_blk(rr * jnp.exp(cu))], axis=0),
               kk, cum, wl, r)
    rhs = each(lambda x, y, e: jnp.concatenate([to_blk(x * e), to_blk(y * e)], axis=0), kka, ke, e_neg)
    ends = each(lambda x, y, e: jnp.concatenate([to_blk(x * e), to_blk(y * e)], axis=0), kka, ke, e_end)
    vb = each(to_blk, v)
    cross = each(lambda a, b: lax.dot_general(a, b, NT_DIMS, preferred_element_type=F32), lhs, rhs)
    nab = each(lambda x: jnp.where(strict, x[:rows, :rows], 0.0), cross)
    aak = each(lambda x: jnp.where(strict, x[:rows, rows:], 0.0).astype(BF16), cross)
    rbk = each(lambda x: jnp.concatenate([jnp.where(incl, x[rows:, :rows], 0.0),
                                          jnp.where(incl, x[rows:, rows:], 0.0)], axis=1).astype(BF16), cross)
    inv = each(lambda x: eye + x, nab)
    pw = each(lambda x: x.astype(BF16), nab)
    for _ in range(max(1, int(math.ceil(math.log2(n))) - 1)):
        pw = each(lambda x: bdot(x, x).astype(BF16), pw)
        inv = each(lambda x, p: x + bdot(x.astype(BF16), p), inv, pw)
    inv = each(lambda x: x.astype(BF16), inv)
    akv = each(bdot, aak, vb)
    bonus = each(lambda rr, x, w, vv: head_sum(rr * x * w) * vv, r, ke, par(rk_ref), v)
    decay_end = each(jnp.exp, cum_last)
    ys = [None] * len(ids)
    for c in range(n_chunks):
        idx = [i for i, (cc, _) in enumerate(ids) if cc == c]
        take = lambda col: [col[i] for i in idx]
        s0 = [st_sc[ids[i][1]] for i in idx]
        ls = each(lambda a, s: lax.dot_general(a, s.astype(BF16), NT_DIMS, preferred_element_type=F32), take(lhs), s0)
        u = each(lambda m, x, y: bdot(m, (x[:rows] + y).astype(BF16)), take(inv), ls, take(akv))
        uv = each(lambda x, y: jnp.concatenate([x.astype(BF16), y], axis=0), u, take(vb))
        y = each(lambda x, m, z: from_blk(x[rows:] + bdot(m, z)), ls, take(rbk), uv)
        new = each(lambda s, d, z, e: s * d + lax.dot_general(z, e, TN_DIMS, preferred_element_type=F32),
                   s0, take(decay_end), uv, take(ends))
        for i, st, yy in zip(idx, new, y):
            st_sc[ids[i][1]] = st
            ys[i] = yy
    mu = each(lambda x: head_sum(x) * (1.0 / D_HEAD), ys)
    yc = each(lambda x, m: x - m, ys, mu)
    var = each(lambda x: head_sum(x * x) * (1.0 / D_HEAD), yc)
    yn = each(lambda x, s, w, b: x * lax.rsqrt(s + gn_eps) * w + b, yc, var, par(gw_ref), par(gb_ref))
    for sl, ln, x, b in zip(sls, lns, yn, bonus):
        o_ref[sl, ln] = ((x + b) * g_ref[sl, ln]).astype(o_ref.dtype)


def rwkv_scan(r, k, v, wl, a, g, k_k, k_a, r_k, gn_w, gn_b, batch, seq):
    t, width = r.shape
    hp = RWKV_HEADS_PER_STEP
    hw = hp * D_HEAD
    gw = _pick(width, (RWKV_GROUPS_PER_STEP * hw, hw))
    ch = RWKV_CHUNK
    tb = _pick(seq, (2 * ch, ch))
    nc = seq // tb
    xspec = pl.BlockSpec((tb, gw), lambda b, h, c: (b * nc + c, h))
    pspec = pl.BlockSpec((1, gw), lambda b, h, c: (0, h))
    vec = lambda p: p.reshape(1, width).astype(F32)
    return pl.pallas_call(
        functools.partial(_rwkv_scan_kernel, heads=hp, chunk=ch, gn_eps=D_GN_EPS),
        grid=(batch, width // gw, nc),
        in_specs=[xspec] * 6 + [pspec] * 5,
        out_specs=xspec,
        out_shape=jax.ShapeDtypeStruct((t, width), BF16),
        scratch_shapes=[pltpu.VMEM((gw // hw, hw, hw), F32)],
        compiler_params=_params("parallel", "parallel", "arbitrary"),
        name="rwkv_scan",
    )(r, k, v, wl, a, g, vec(k_k), vec(k_a), vec(r_k), vec(gn_w), vec(gn_b))


def _extract_top(work, dst_ref, count, want_rank=False):
    rank = jnp.full(work.shape, float(count), F32) if want_rank else None
    for r in range(count):
        m = jnp.max(work, axis=0, keepdims=True)
        dst_ref[r:r + 1, :] = m
        hit = work >= m
        if want_rank:
            rank = jnp.where(hit, float(r), rank)
        work = jnp.where(hit, -jnp.inf, work)
    return rank


def _peer_topk_kernel(q_ref, keys_ref, n1_ref, f1_ref, rk2_ref, e2_ref, a1_sc, a2_sc, cand_sc, top_sc):
    kk = P_TOPK
    for h in range(P_HEADS):
        sc = []
        for half in range(2):
            qh = q_ref[:, (2 * h + half) * LANES:(2 * h + half + 1) * LANES]
            sc.append(lax.dot_general(keys_ref[h, half], qh, NT_DIMS, precision=HIGHEST,
                                      preferred_element_type=F32))
        _extract_top(sc[0], a1_sc, kk)
        rank2 = _extract_top(sc[1], a2_sc, kk, want_rank=True)
        cand_sc[...] = jnp.full(cand_sc.shape, -jnp.inf, F32)
        off = 0
        for i in range(kk):
            nj = kk // (i + 1)
            cand_sc[off:off + nj, :] = a1_sc[i:i + 1, :] + a2_sc[0:nj, :]
            off += nj
        _extract_top(cand_sc[...], top_sc, kk)
        top = top_sc[...]
        best = top[0:1, :]
        tau = top[kk - 1:kk, :]
        zsum = jnp.sum(jnp.exp(top - best), axis=0, keepdims=True)
        count = jnp.zeros(sc[0].shape, F32)
        for j in range(kk):
            count = count + jnp.where(sc[0] + a2_sc[j:j + 1, :] >= tau, 1.0, 0.0)
        n1_ref[h] = count
        rk2_ref[h] = rank2.astype(BF16)
        f1_ref[h] = jnp.exp(sc[0] - a1_sc[0:1, :]) / zsum
        e2_ref[h] = jnp.exp(sc[1] - a2_sc[0:1, :]).astype(BF16)


def peer_topk(q, keys):
    t = q.shape[0]
    tt = _pick(t, (256, 128))
    n_cand = sum(P_TOPK // (i + 1) for i in range(P_TOPK))
    n_cand = -(-n_cand // 8) * 8
    big = pl.BlockSpec((P_HEADS, P_NKEYS, tt), lambda i: (0, 0, i))
    big_shape = jax.ShapeDtypeStruct((P_HEADS, P_NKEYS, t), F32)
    return pl.pallas_call(
        _peer_topk_kernel,
        grid=(t // tt,),
        in_specs=[pl.BlockSpec((tt, q.shape[1]), lambda i: (i, 0)),
                  pl.BlockSpec(keys.shape, lambda i: (0, 0, 0, 0))],
        out_specs=[big, big, big, big],
        out_shape=[big_shape, big_shape, big_shape.update(dtype=BF16), big_shape.update(dtype=BF16)],
        scratch_shapes=[pltpu.VMEM((P_TOPK, tt), F32), pltpu.VMEM((P_TOPK, tt), F32),
                        pltpu.VMEM((n_cand, tt), F32), pltpu.VMEM((P_TOPK, tt), F32)],
        compiler_params=_params("parallel"),
        name="peer_topk",
    )(q, keys)


def _peer_dense_kernel(sc_ref, x_ref, u_ref, v_ref, n1_ref, f1_ref, rk2_ref, e2_ref, o_ref, *, rows, chains):
    @pl.when(pl.program_id(1) == 0)
    def _():
        o_ref[...] = jnp.zeros(o_ref.shape, F32)

    tc = x_ref.shape[0] // chains
    hr = rows // 2
    dh = o_ref.shape[1] // 2
    tok = lambda ch: slice(ch * tc, (ch + 1) * tc)

    def activation(ch, half):
        es = slice(half * hr * P_NKEYS, (half + 1) * hr * P_NKEYS)
        return lax.dot_general(u_ref[es, :], x_ref[tok(ch), :], NT_DIMS, preferred_element_type=F32)

    def gated(ch, half, act):
        ts = tok(ch)
        sc = sc_ref[0]
        act = (act * (0.5 * sc)) * (1.0 + lax.erf(act * (sc * 2.0 ** -0.5)))
        act = act.astype(BF16)
        pk = P_NKEYS // BF16_ROWS
        parts = []
        for r in range(hr):
            row = half * hr + r
            coef = None
            for h in range(P_HEADS):
                n_row = jnp.broadcast_to(n1_ref[h, row:row + 1, ts], (BF16_ROWS, tc)).astype(BF16)
                f_row = jnp.broadcast_to(f1_ref[h, row:row + 1, ts], (BF16_ROWS, tc)).astype(BF16)
                sel = rk2_ref[h, :, :, ts] < n_row[None]
                c = jnp.where(sel, e2_ref[h, :, :, ts] * f_row[None], jnp.zeros((), BF16))
                coef = c if coef is None else coef + c
            a3 = act[r * P_NKEYS:(r + 1) * P_NKEYS, :].reshape(pk, BF16_ROWS, tc)
            parts.append((coef * a3).reshape(P_NKEYS, tc))
        return parts

    def project(ch, half, w):
        cs = slice(half * dh, (half + 1) * dh)
        o_ref[tok(ch), cs] += lax.dot_general(w, v_ref[:, cs], TN_DIMS, preferred_element_type=F32)

    acts = [activation(0, 0), activation(0, 1)]
    w_prev = None
    for ch in range(chains):
        nxt, parts = [], []
        for half in range(2):
            parts += gated(ch, half, acts[half])
            if ch + 1 < chains:
                nxt.append(activation(ch + 1, half))
            if w_prev is not None:
                project(ch - 1, half, w_prev)
        w_prev = jnp.concatenate(parts, axis=0)
        acts = nxt
    project(chains - 1, 0, w_prev)
    project(chains - 1, 1, w_prev)


def peer_dense(x, u, v, act_scale, n1, f1, rk2, e2):
    t, d = x.shape
    n = u.shape[0]
    tt = _pick(t, (1024, 512, 256, 128))
    chains = 2 if tt % 256 == 0 else 1
    rows = 4
    tn = rows * P_NKEYS
    n1r = n1.reshape(P_HEADS, P_NKEYS // rows, rows, t)
    f1r = f1.reshape(P_HEADS, P_NKEYS // rows, rows, t)
    pk = P_NKEYS // BF16_ROWS
    rk2 = rk2.reshape(P_HEADS, pk, BF16_ROWS, t)
    e2 = e2.reshape(P_HEADS, pk, BF16_ROWS, t)
    once = pl.Buffered(1)
    rowspec = pl.BlockSpec((P_HEADS, None, rows, tt), lambda i, e: (0, e, 0, i))
    fullspec = pl.BlockSpec((P_HEADS, pk, BF16_ROWS, tt), lambda i, e: (0, 0, 0, i), pipeline_mode=once)
    return pl.pallas_call(
        functools.partial(_peer_dense_kernel, rows=rows, chains=chains),
        grid=(t // tt, n // tn),
        in_specs=[pl.BlockSpec(memory_space=pltpu.SMEM),
                  pl.BlockSpec((tt, d), lambda i, e: (i, 0), pipeline_mode=once),
                  pl.BlockSpec((tn, d), lambda i, e: (e, 0)),
                  pl.BlockSpec((tn, d), lambda i, e: (e, 0)),
                  rowspec, rowspec, fullspec, fullspec],
        out_specs=pl.BlockSpec((tt, d), lambda i, e: (i, 0), pipeline_mode=once),
        out_shape=jax.ShapeDtypeStruct((t, d), F32),
        compiler_params=_params("parallel", "arbitrary"),
        name="peer_dense",
    )(act_scale.astype(F32), x, u, v, n1r, f1r, rk2, e2)


def _fp8_scaled(a):
    amax = jnp.max(jnp.abs(a)).astype(F32)
    k = jnp.floor(jnp.log2(FP8_TARGET_MAX / jnp.maximum(amax, 1e-30)))
    scale = jnp.exp2(k)
    return (a.astype(F32) * scale).astype(FP8), 1.0 / scale


def peer_ffn(hn, w_q, keys, u, v):
    q = matmul([hn], [w_q.astype(BF16)], F32)
    n1, f1, rk2, e2 = peer_topk(q, keys)
    x8, inv_x = _fp8_scaled(hn)
    u8, inv_u = _fp8_scaled(u)
    return peer_dense(x8, u8, v.astype(BF16), (inv_x * inv_u).reshape(1), n1, f1, rk2, e2)


def even_mixer(hn, w_in, w_out, lam_params, subln_gain, lam_init, w_gate, b_gate, gla_norm, batch, seq, residual=None):
    na = w_out.shape[0] // 2
    a_heads = na // A_V_DIM
    b_heads = na // B_V_DIM
    sizes = (2 * a_heads * A_QK_DIM, 2 * a_heads * A_QK_DIM, na, b_heads * B_K_DIM, b_heads * B_K_DIM, na,
             w_gate.shape[0], na)
    names = ("aq", "ak", "av", "bq", "bk", "bv", "lo", "og")
    start, off = {}, 0
    for nm, sz in zip(names, sizes):
        start[nm] = off
        off += sz
    w_main = jnp.concatenate([w_in[:, :start["lo"]], w_in[:, start["og"]:]], axis=1).astype(BF16)
    cols = dict(start)
    cols["og"] = start["lo"]
    w_lo = _pad_cols(w_in[:, start["lo"]:start["og"]], LANES).astype(BF16)
    p = matmul([hn], [w_main], BF16)
    p_lo = matmul([hn], [w_lo], F32)
    qk = rope(p, rope_tables(seq, A_ROT), seq, width=start["av"], col_block=0, half=A_ROT // 2)
    lp = lam_params.astype(F32)
    lam = jnp.exp(jnp.sum(lp[0] * lp[1])) - jnp.exp(jnp.sum(lp[2] * lp[3])) + lam_init
    o_a = diff_attention(qk, p, lam, subln_gain, lam_init, batch, seq, start["av"])
    w_gate_p = jnp.pad(w_gate.astype(F32), ((0, LANES - w_gate.shape[0]), (0, 0)))
    o_b = gla(p, p_lo, w_gate_p, b_gate.astype(F32), gla_norm.astype(F32), batch, seq, cols)
    w_out = w_out.astype(BF16)
    return matmul([o_a, o_b], [w_out[:na], w_out[na:]], F32, residual)


def odd_mixer(hn, w_in, w_out, q_norm, kv_norm, w_uq, w_ukv, mu, w0, w2, a0, a2, g2,
              k_k, k_a, r_k, gn_w, gn_b, batch, seq, residual=None):
    q_rank, kv_rank = q_norm.shape[0], kv_norm.shape[0]
    width = w0.shape[0]
    rw, ra, rg = w2.shape[0], a2.shape[0], g2.shape[0]
    c_cols = q_rank + kv_rank + C_ROPE
    heads = w_uq.shape[1] // (C_NOPE + C_ROPE)
    c_pad = -(-(q_rank + kv_rank + LANES) // 256) * 256
    w_c = _pad_cols(w_in[:, :c_cols], c_pad).astype(BF16)
    pad_l = lambda m: -(-m // LANES) * LANES
    d0 = c_cols
    segs, mus, off = [], [], d0
    for sz in (width, width, width, rw, ra, rg):
        segs.append(_pad_cols(w_in[:, off:off + sz], pad_l(sz)))
        mus.append(jnp.pad(mu[off - d0:off - d0 + sz], (0, pad_l(sz) - sz)))
        off += sz
    w_d = jnp.concatenate(segs, axis=1).astype(BF16)
    mu_d = jnp.concatenate(mus).astype(F32)
    pad_r = lambda w: jnp.pad(w.astype(F32), ((0, pad_l(w.shape[0]) - w.shape[0]), (0, 0)))

    p_c = matmul([hn], [w_c], F32)
    p_d = matmul([hn], [w_d], F32)

    cq = add_rmsnorm([p_c], q_norm, BF16, width=q_rank, col_block=0)
    ckv = add_rmsnorm([p_c], kv_norm, BF16, width=kv_rank, col_block=q_rank // kv_rank)
    wq3 = w_uq.reshape(q_rank, heads, C_NOPE + C_ROPE)
    wq_nope = wq3[:, :, :C_NOPE].reshape(q_rank, heads * C_NOPE)
    wq_rope = jnp.pad(wq3[:, :, C_NOPE:], ((0, 0), (0, 0), (0, LANES - C_ROPE))).reshape(q_rank, heads * LANES)
    q_all = matmul([cq], [jnp.concatenate([wq_nope, wq_rope], axis=1).astype(BF16)], BF16)
    wkv3 = w_ukv.reshape(kv_rank, heads, C_NOPE + C_V)
    w_kv = jnp.concatenate([wkv3[:, :, :C_NOPE].reshape(kv_rank, heads * C_NOPE),
                            wkv3[:, :, C_NOPE:].reshape(kv_rank, heads * C_V)], axis=1).astype(BF16)
    kv = matmul([ckv], [w_kv], BF16)
    tables = rope_tables(seq, C_ROPE)
    q_rope = rope(q_all, tables, seq, width=heads * LANES, col_block=1, half=C_ROPE // 2)
    k_rope = rope(p_c, tables, seq, width=LANES, col_block=(q_rank + kv_rank) // LANES, half=C_ROPE // 2)
    o_c = mla_attention(q_all, q_rope, kv, k_rope, batch, seq)

    r, k, v, wl, a, g = rwkv_prep(p_d, mu_d, w0.astype(F32), pad_r(w2), a0.astype(F32), pad_r(a2), pad_r(g2),
                                  batch, seq)
    o_d = rwkv_scan(r, k, v, wl, a, g, k_k, k_a, r_k, gn_w, gn_b, batch, seq)
    nc = o_c.shape[1]
    w_out = w_out.astype(BF16)
    return matmul([o_c, o_d], [w_out[:nc], w_out[nc:]], F32, residual)


def kernel(x, norm_mix, norm_ffn, norm_final, even_w_in, even_w_out, diff_lambda, diff_subln, gla_w_gate, gla_b_gate, gla_norm, odd_w_in, odd_w_out, mla_q_norm, mla_kv_norm, mla_w_uq, mla_w_ukv, rwkv_mu, rwkv_w0, rwkv_w2, rwkv_a0, rwkv_a2, rwkv_g2, rwkv_k_k, rwkv_k_a, rwkv_r_k, rwkv_gn_w, rwkv_gn_b, peer_w_q, peer_keys, peer_u, peer_v):
    batch, seq, d = x.shape
    depth = norm_mix.shape[0]
    h = x.reshape(batch * seq, d)
    pending = []
    for layer in range(depth):
        j = layer // 2
        if pending:
            h, hn = add_rmsnorm([h] + pending, norm_mix[layer], BF16, emit_sum=True)
        else:
            hn = add_rmsnorm([h], norm_mix[layer], BF16)
        if layer % 2 == 0:
            lam_init = 0.8 - 0.6 * math.exp(-0.3 * layer)
            h = even_mixer(hn, even_w_in[j], even_w_out[j], diff_lambda[j], diff_subln[j], lam_init,
                           gla_w_gate[j], gla_b_gate[j], gla_norm[j], batch, seq, residual=h)
        else:
            h = odd_mixer(hn, odd_w_in[j], odd_w_out[j], mla_q_norm[j], mla_kv_norm[j], mla_w_uq[j],
                          mla_w_ukv[j], rwkv_mu[j], rwkv_w0[j], rwkv_w2[j], rwkv_a0[j], rwkv_a2[j],
                          rwkv_g2[j], rwkv_k_k[j], rwkv_k_a[j], rwkv_r_k[j], rwkv_gn_w[j], rwkv_gn_b[j],
                          batch, seq, residual=h)
        hn2 = add_rmsnorm([h], norm_ffn[layer], BF16)
        pending = [peer_ffn(hn2, peer_w_q[layer], peer_keys[layer], peer_u[layer], peer_v[layer])]
    out = add_rmsnorm([h] + pending, norm_final, F32)
    return out.reshape(batch, seq, d).astype(x.dtype)
```

```python
import functools
import math

import jax
import jax.numpy as jnp
from jax import lax
from jax.experimental import pallas as pl
from jax.experimental.pallas import tpu as pltpu

F32 = jnp.float32
BF16 = jnp.bfloat16
FP8 = jnp.float8_e4m3fn
FP8_TARGET_MAX = 256.0
HIGHEST = lax.Precision.HIGHEST

LANES = 128
BF16_ROWS = 16
VMEM_LIMIT = 56 * 1024 * 1024

EPS = 1e-6
ROPE_THETA = 500000.0
A_QK_DIM = 128
A_V_DIM = 256
A_ROT = A_QK_DIM // 4
B_V_DIM = 512
B_K_DIM = 256
B_GATE_TAU = 16.0
GLA_CHUNK = 64
C_NOPE = 128
C_ROPE = 64
C_V = 128
D_HEAD = 64
D_GN_EPS = 64e-5
RWKV_CHUNK = 64
ATTN_BLOCKS = (1024, 512, 256, 128)
ATTN_SUB = 4
RWKV_HEADS_PER_STEP = 4
RWKV_GROUPS_PER_STEP = 4
P_HEADS = 8
P_NKEYS = 128
P_TOPK = 16

NT_DIMS = (((1,), (1,)), ((), ()))
TN_DIMS = (((0,), (0,)), ((), ()))


def _params(*sem):
    return pltpu.CompilerParams(dimension_semantics=sem, vmem_limit_bytes=VMEM_LIMIT)


def _pick(n, cands):
    for c in cands:
        if n % c == 0:
            return c
    raise ValueError(f"no tile in {cands} divides {n}")


def _pad_cols(w, n):
    return jnp.pad(w, ((0, 0), (0, n - w.shape[1])))


def _norm_kernel(*refs, eps, n_add, emit_sum):
    xs = refs[:n_add]
    g_ref = refs[n_add]
    outs = refs[n_add + 1:]
    x = xs[0][...].astype(F32)
    for r in xs[1:]:
        x = x + r[...].astype(F32)
    y = x * lax.rsqrt(jnp.mean(x * x, axis=-1, keepdims=True) + eps) * g_ref[...]
    if emit_sum:
        outs[0][...] = x
        outs[1][...] = y.astype(outs[1].dtype)
    else:
        outs[0][...] = y.astype(outs[0].dtype)


def add_rmsnorm(xs, g, out_dtype, *, emit_sum=False, width=None, col_block=0, eps=EPS):
    t = xs[0].shape[0]
    width = width or xs[0].shape[1]
    tr = _pick(t, (128, 64, 32, 16, 8))
    spec = pl.BlockSpec((tr, width), lambda i: (i, col_block))
    ospec = pl.BlockSpec((tr, width), lambda i: (i, 0))
    out_shape = [jax.ShapeDtypeStruct((t, width), out_dtype)]
    out_specs = [ospec]
    if emit_sum:
        out_shape = [jax.ShapeDtypeStruct((t, width), F32)] + out_shape
        out_specs = [ospec, ospec]
    res = pl.pallas_call(
        functools.partial(_norm_kernel, eps=eps, n_add=len(xs), emit_sum=emit_sum),
        grid=(t // tr,),
        in_specs=[spec] * len(xs) + [pl.BlockSpec((1, width), lambda i: (0, 0))],
        out_specs=out_specs,
        out_shape=out_shape,
        compiler_params=_params("parallel"),
        name="add_rmsnorm",
    )(*xs, g.reshape(1, width).astype(F32))
    return res if emit_sum else res[0]


def _mm_kernel(*refs, n_in, has_residual):
    o_ref = refs[-1]
    acc = jnp.dot(refs[0][...], refs[n_in][...], preferred_element_type=F32)
    for i in range(1, n_in):
        acc = acc + jnp.dot(refs[i][...], refs[n_in + i][...], preferred_element_type=F32)
    if has_residual:
        acc = acc + refs[2 * n_in][...]
    o_ref[...] = acc.astype(o_ref.dtype)


def matmul(xs, ws, out_dtype, residual=None, stacked=False):
    m = xs[0].shape[0]
    if stacked:
        kw = xs[0].shape[1]
        ws = [ws] * len(xs)
    n = ws[0].shape[1]
    tm = _pick(m, (1024, 512, 256, 128, 64, 32, 16))
    tn = _pick(n, (512, 768, 640, 256, 128))
    in_specs = [pl.BlockSpec((tm, x.shape[1]), lambda i, j: (i, 0)) for x in xs]
    if stacked:
        in_specs += [pl.BlockSpec((kw, tn), functools.partial(lambda i, j, b: (b, j), b=b)) for b in range(len(xs))]
    else:
        in_specs += [pl.BlockSpec((w.shape[0], tn), lambda i, j: (0, j)) for w in ws]
    extra = []
    if residual is not None:
        in_specs.append(pl.BlockSpec((tm, tn), lambda i, j: (i, j)))
        extra = [residual]
    return pl.pallas_call(
        functools.partial(_mm_kernel, n_in=len(xs), has_residual=residual is not None),
        grid=(m // tm, n // tn),
        in_specs=in_specs,
        out_specs=pl.BlockSpec((tm, tn), lambda i, j: (i, j)),
        out_shape=jax.ShapeDtypeStruct((m, n), out_dtype),
        compiler_params=_params("parallel", "arbitrary"),
        name="matmul",
    )(*xs, *ws, *extra)


def rope_tables(seq, rot_dim):
    half = rot_dim // 2
    inv = 1.0 / (ROPE_THETA ** (jnp.arange(0, rot_dim, 2, dtype=F32) / rot_dim))
    ang = jnp.arange(seq, dtype=F32)[:, None] * inv[None, :]
    cos, sin = jnp.cos(ang), jnp.sin(ang)
    z = jnp.zeros((seq, LANES - 2 * half), F32)
    zh = jnp.zeros((seq, half), F32)
    c = jnp.concatenate([cos, cos, jnp.ones_like(z)], axis=1)
    s_up = jnp.concatenate([zh, sin, z], axis=1)
    s_dn = jnp.concatenate([-sin, zh, z], axis=1)
    return c, s_up, s_dn


def _rope_kernel(x_ref, c_ref, su_ref, sd_ref, o_ref, *, half, groups):
    c, su, sd = c_ref[...], su_ref[...], sd_ref[...]
    for g in range(groups):
        sl = slice(g * LANES, (g + 1) * LANES)
        x = x_ref[:, sl].astype(F32)
        y = x * c + pltpu.roll(x, half, 1) * su + pltpu.roll(x, LANES - half, 1) * sd
        o_ref[:, sl] = y.astype(o_ref.dtype)


def rope(x, tables, seq, *, width, col_block, half):
    t = x.shape[0]
    tr = _pick(seq, (256, 128, 64, 32, 16))
    nb = seq // tr
    tspec = pl.BlockSpec((tr, LANES), lambda i: (i % nb, 0))
    return pl.pallas_call(
        functools.partial(_rope_kernel, half=half, groups=width // LANES),
        grid=(t // tr,),
        in_specs=[pl.BlockSpec((tr, width), lambda i: (i, col_block)), tspec, tspec, tspec],
        out_specs=pl.BlockSpec((tr, width), lambda i: (i, 0)),
        out_shape=jax.ShapeDtypeStruct((t, width), BF16),
        compiler_params=_params("parallel"),
        name="rope",
    )(x, *tables)


LOG2E = 1.4426950408889634


def _scaled_q(q, scale):
    return (q.astype(F32) * (scale * LOG2E)).astype(BF16)


def _softmax_chains(chains):
    s = [lax.dot_general(q, k, NT_DIMS, preferred_element_type=F32) for q, k, *_ in chains]
    def masked(x, tail):
        if tail is None:
            return x
        w = tail.shape[1]
        last = jnp.where(tail, x[:, x.shape[1] - w:], -jnp.inf)
        return last if w == x.shape[1] else jnp.concatenate([x[:, :x.shape[1] - w], last], axis=1)

    s = [masked(x, c[6]) for x, c in zip(s, chains)]
    m_prev = [c[3][...] for c in chains]
    m_new = [jnp.maximum(mp, jnp.max(x, axis=-1, keepdims=True)) for mp, x in zip(m_prev, s)]
    alpha = [jnp.exp2(mp - mn) for mp, mn in zip(m_prev, m_new)]
    p = [jnp.exp2(x - mn) for x, mn in zip(s, m_new)]
    pv = [jnp.dot(x.astype(c[2].dtype), c[2], preferred_element_type=F32) for x, c in zip(p, chains)]
    for c, a, x, y, mn in zip(chains, alpha, p, pv, m_new):
        c[4][...] = a * c[4][...] + jnp.sum(x, axis=-1, keepdims=True)
        c[5][...] = a * c[5][...] + y
        c[3][...] = mn


def _causal_sweep(i, tq, step):
    def body(j, carry):
        step(pl.ds(pl.multiple_of(j * tq, tq), tq), False)
        return carry
    lax.fori_loop(0, i, body, 0)
    step(pl.ds(pl.multiple_of(i * tq, tq), tq), True)


def _row_groups(q, k, v, m_sc, l_sc, acc_sc, diagonal):
    rs = q.shape[0] // ATTN_SUB
    chains = []
    tri = _tril(rs)
    for g in range(ATTN_SUB):
        qg = q[g * rs:(g + 1) * rs]
        if diagonal:
            r = g * rs + lax.broadcasted_iota(jnp.int32, (rs, k.shape[0]), 0)
            c = lax.broadcasted_iota(jnp.int32, (rs, k.shape[0]), 1)
            chains.append((qg, k, v, m_sc.at[g], l_sc.at[g], acc_sc.at[g], c <= r))
        else:
            chains.append((qg, k, v, m_sc.at[g], l_sc.at[g], acc_sc.at[g], None))
    return chains


def _diff_attn_kernel(lam_ref, q_ref, k_ref, v_ref, g_ref, o_ref, m_sc, l_sc, acc_sc, *, scale, tq, out_scale, eps):
    i = pl.program_id(2)
    m_sc[...] = jnp.full(m_sc.shape, -jnp.inf, F32)
    l_sc[...] = jnp.zeros(l_sc.shape, F32)
    acc_sc[...] = jnp.zeros(acc_sc.shape, F32)
    qs = [_scaled_q(q_ref[:, mp * A_QK_DIM:(mp + 1) * A_QK_DIM], scale) for mp in range(2)]

    def step(rows, diagonal):
        v = v_ref[rows, :]
        chains = []
        for mp in range(2):
            k = k_ref[rows, mp * A_QK_DIM:(mp + 1) * A_QK_DIM]
            chains += _row_groups(qs[mp], k, v, m_sc.at[mp], l_sc.at[mp], acc_sc.at[mp], diagonal)
        _softmax_chains(chains)

    _causal_sweep(i, tq, step)
    rs = tq // ATTN_SUB
    for g in range(ATTN_SUB):
        o = acc_sc[0, g] / l_sc[0, g] - lam_ref[0] * (acc_sc[1, g] / l_sc[1, g])
        o = o * lax.rsqrt(jnp.mean(o * o, axis=-1, keepdims=True) + eps) * g_ref[...] * out_scale
        o_ref[g * rs:(g + 1) * rs, :] = o.astype(o_ref.dtype)


def diff_attention(qk, p, lam, subln_gain, lam_init, batch, seq, v_col):
    t = qk.shape[0]
    heads = qk.shape[1] // (4 * A_QK_DIM)
    hw = 2 * A_QK_DIM
    tq = _pick(seq, ATTN_BLOCKS)
    rs = tq // ATTN_SUB
    nq = seq // tq
    vb = v_col // A_V_DIM
    kern = functools.partial(_diff_attn_kernel, scale=A_QK_DIM ** -0.5, tq=tq, out_scale=1.0 - lam_init, eps=EPS)
    return pl.pallas_call(
        kern,
        grid=(batch, heads, nq),
        in_specs=[
            pl.BlockSpec(memory_space=pltpu.SMEM),
            pl.BlockSpec((tq, hw), lambda b, h, i: (b * nq + i, h)),
            pl.BlockSpec((seq, hw), lambda b, h, i: (b, heads + h)),
            pl.BlockSpec((seq, A_V_DIM), lambda b, h, i: (b, vb + h)),
            pl.BlockSpec((1, A_V_DIM), lambda b, h, i: (0, 0)),
        ],
        out_specs=pl.BlockSpec((tq, A_V_DIM), lambda b, h, i: (b * nq + i, h)),
        out_shape=jax.ShapeDtypeStruct((t, heads * A_V_DIM), BF16),
        scratch_shapes=[pltpu.VMEM((2, ATTN_SUB, rs, 1), F32), pltpu.VMEM((2, ATTN_SUB, rs, 1), F32),
                        pltpu.VMEM((2, ATTN_SUB, rs, A_V_DIM), F32)],
        compiler_params=_params("parallel", "parallel", "arbitrary"),
        name="diff_attention",
    )(lam.reshape(1).astype(F32), qk, qk, p, subln_gain.reshape(1, A_V_DIM).astype(F32))


def _mla_attn_kernel(qn_ref, qr_ref, kn_ref, kr_ref, v_ref, o_ref, m_sc, l_sc, acc_sc, *, scale, tq):
    i = pl.program_id(2)
    m_sc[...] = jnp.full(m_sc.shape, -jnp.inf, F32)
    l_sc[...] = jnp.zeros(l_sc.shape, F32)
    acc_sc[...] = jnp.zeros(acc_sc.shape, F32)
    q = _scaled_q(jnp.concatenate([qn_ref[...], qr_ref[...]], axis=-1), scale)

    def step(rows, diagonal):
        k = jnp.concatenate([kn_ref[rows, :], kr_ref[rows, :]], axis=-1)
        _softmax_chains(_row_groups(q, k, v_ref[rows, :], m_sc, l_sc, acc_sc, diagonal))

    _causal_sweep(i, tq, step)
    rs = tq // ATTN_SUB
    for g in range(ATTN_SUB):
        o_ref[g * rs:(g + 1) * rs, :] = (acc_sc[g] / l_sc[g]).astype(o_ref.dtype)


def mla_attention(q_all, q_rope, kv, k_rope, batch, seq):
    t = kv.shape[0]
    heads = kv.shape[1] // (C_NOPE + C_V)
    tq = _pick(seq, ATTN_BLOCKS)
    rs = tq // ATTN_SUB
    nq = seq // tq
    qmap = lambda b, h, i: (b * nq + i, h)
    return pl.pallas_call(
        functools.partial(_mla_attn_kernel, scale=(C_NOPE + C_ROPE) ** -0.5, tq=tq),
        grid=(batch, heads, nq),
        in_specs=[
            pl.BlockSpec((tq, C_NOPE), qmap),
            pl.BlockSpec((tq, LANES), qmap),
            pl.BlockSpec((seq, C_NOPE), lambda b, h, i: (b, h)),
            pl.BlockSpec((seq, LANES), lambda b, h, i: (b, 0)),
            pl.BlockSpec((seq, C_V), lambda b, h, i: (b, heads + h)),
        ],
        out_specs=pl.BlockSpec((tq, C_V), qmap),
        out_shape=jax.ShapeDtypeStruct((t, heads * C_V), BF16),
        scratch_shapes=[pltpu.VMEM((ATTN_SUB, rs, 1), F32), pltpu.VMEM((ATTN_SUB, rs, 1), F32),
                        pltpu.VMEM((ATTN_SUB, rs, C_V), F32)],
        compiler_params=_params("parallel", "parallel", "arbitrary"),
        name="mla_attention",
    )(q_all, q_rope, kv, k_rope, kv)


def _tril(n, strict=False):
    r = lax.broadcasted_iota(jnp.int32, (n, n), 0)
    c = lax.broadcasted_iota(jnp.int32, (n, n), 1)
    return (r > c) if strict else (r >= c)


def _gla_kernel(q_ref, k_ref, v_ref, og_ref, lo_ref, wg_ref, bg_ref, gn_ref, o_ref, st_sc, *, scale, eps, chunk, heads):
    @pl.when(pl.program_id(1) == 0)
    def _():
        st_sc[...] = jnp.zeros(st_sc.shape, F32)

    n = chunk
    n_chunks = q_ref.shape[0] // n
    dk, dv = B_K_DIM, B_V_DIM
    each = lambda f, *cols: [f(*a) for a in zip(*cols)]
    tri = _tril(n)
    trif = tri.astype(F32)
    rows = [slice(c * n, (c + 1) * n) for c in range(n_chunks)]
    z = [jnp.dot(lo_ref[r, :], wg_ref[...], precision=HIGHEST, preferred_element_type=F32) + bg_ref[...] for r in rows]
    log_a = each(lambda x: (jnp.minimum(x, 0.0) - jnp.log1p(jnp.exp(-jnp.abs(x)))) / B_GATE_TAU, z)
    b_all = each(lambda x: jnp.dot(trif, x, precision=HIGHEST, preferred_element_type=F32), log_a)
    ids = [(c, h) for c in range(n_chunks) for h in range(heads)]
    kl = [slice(h * dk, (h + 1) * dk) for _, h in ids]
    vl = [slice(h * dv, (h + 1) * dv) for _, h in ids]
    rw = [rows[c] for c, _ in ids]
    b = [b_all[c][:, s] for (c, _), s in zip(ids, kl)]
    b_last = each(lambda x: x[n - 1:n, :], b)
    q = [q_ref[r, s].astype(F32) * scale for r, s in zip(rw, kl)]
    k = [k_ref[r, s].astype(F32) for r, s in zip(rw, kl)]
    v = [v_ref[r, s] for r, s in zip(rw, vl)]
    qe = each(lambda x, bb: (x * jnp.exp(bb)).astype(BF16), q, b)
    ke = each(lambda x, bb: (x * jnp.exp(-bb)).astype(BF16), k, b)
    kend = each(lambda x, bb, bl: (x * jnp.exp(bl - bb)).astype(BF16), k, b, b_last)
    attn = each(lambda a, c: jnp.where(tri, lax.dot_general(a, c, NT_DIMS, preferred_element_type=F32), 0.0).astype(BF16),
                qe, ke)
    o_intra = each(lambda a, vv: jnp.dot(a, vv, preferred_element_type=F32), attn, v)
    decay = each(jnp.exp, b_last)
    outs = [None] * len(ids)
    for c in range(n_chunks):
        idx = [i for i, (cc, _) in enumerate(ids) if cc == c]
        st = [st_sc[ids[i][1]] for i in idx]
        o_inter = [lax.dot_general(qe[i], s.astype(BF16), NT_DIMS, preferred_element_type=F32) for i, s in zip(idx, st)]
        new = [s * decay[i] + lax.dot_general(v[i], kend[i], TN_DIMS, preferred_element_type=F32)
               for i, s in zip(idx, st)]
        for i, s_new, oi in zip(idx, new, o_inter):
            st_sc[ids[i][1]] = s_new
            outs[i] = oi + o_intra[i]
    for r, s, o in zip(rw, vl, outs):
        o = o * lax.rsqrt(jnp.mean(o * o, axis=-1, keepdims=True) + eps) * gn_ref[...]
        og = og_ref[r, s].astype(F32)
        o_ref[r, s] = (o * (og * jax.nn.sigmoid(og))).astype(o_ref.dtype)


def gla(p, p_lo, w_gate, b_gate, gn, batch, seq, cols):
    t = p.shape[0]
    heads = w_gate.shape[1] // B_K_DIM
    ch = GLA_CHUNK
    tb = _pick(seq, (2 * ch, ch))
    nb = seq // tb
    kw, vw = heads * B_K_DIM, heads * B_V_DIM
    row = lambda b, c: b * nb + c
    return pl.pallas_call(
        functools.partial(_gla_kernel, scale=B_K_DIM ** -0.5, eps=EPS, chunk=ch, heads=heads),
        grid=(batch, nb),
        in_specs=[
            pl.BlockSpec((tb, kw), lambda b, c: (row(b, c), cols["bq"] // kw)),
            pl.BlockSpec((tb, kw), lambda b, c: (row(b, c), cols["bk"] // kw)),
            pl.BlockSpec((tb, vw), lambda b, c: (row(b, c), cols["bv"] // vw)),
            pl.BlockSpec((tb, vw), lambda b, c: (row(b, c), cols["og"] // vw)),
            pl.BlockSpec((tb, LANES), lambda b, c: (row(b, c), 0)),
            pl.BlockSpec((LANES, kw), lambda b, c: (0, 0)),
            pl.BlockSpec((1, kw), lambda b, c: (0, 0)),
            pl.BlockSpec((1, B_V_DIM), lambda b, c: (0, 0)),
        ],
        out_specs=pl.BlockSpec((tb, vw), lambda b, c: (row(b, c), 0)),
        out_shape=jax.ShapeDtypeStruct((t, vw), BF16),
        scratch_shapes=[pltpu.VMEM((heads, B_V_DIM, B_K_DIM), F32)],
        compiler_params=_params("parallel", "arbitrary"),
        name="gla",
    )(p, p, p, p, p_lo, w_gate, b_gate.reshape(1, -1), gn.reshape(1, -1))


def _rwkv_prep_kernel(pd_ref, mu_ref, w0_ref, w2_ref, a0_ref, a2_ref, g2_ref,
                      r_ref, k_ref, v_ref, wl_ref, a_ref, g_ref, carry_sc, *, width, rw, ra):
    @pl.when(pl.program_id(1) == 0)
    def _():
        carry_sc[...] = jnp.zeros(carry_sc.shape, F32)

    x = pd_ref[...]
    n = x.shape[0]
    first = lax.broadcasted_iota(jnp.int32, x.shape, 0) == 0
    prev = jnp.where(first, carry_sc[...], pltpu.roll(x, 1, 0))
    carry_sc[...] = x[n - 1:n, :]
    xs = x + (prev - x) * mu_ref[...]
    r_ref[...] = xs[:, 0:width]
    k_ref[...] = xs[:, width:2 * width]
    v_ref[...] = xs[:, 2 * width:3 * width]
    c0 = 3 * width
    xw, xa, xg = xs[:, c0:c0 + rw], xs[:, c0 + rw:c0 + rw + ra], xs[:, c0 + rw + ra:]
    zw = w0_ref[...] + jnp.dot(jnp.tanh(xw), w2_ref[...], precision=HIGHEST, preferred_element_type=F32)
    w = -(jnp.maximum(-zw, 0.0) + jnp.log1p(jnp.exp(-jnp.abs(zw)))) - 0.5
    wl_ref[...] = -jnp.exp(w)
    za = a0_ref[...] + jnp.dot(xa, a2_ref[...], precision=HIGHEST, preferred_element_type=F32)
    a_ref[...] = jax.nn.sigmoid(za)
    g_ref[...] = jnp.dot(jax.nn.sigmoid(xg), g2_ref[...], precision=HIGHEST, preferred_element_type=F32)


def rwkv_prep(pd, mu, w0, w2, a0, a2, g2, batch, seq):
    t, cols = pd.shape
    width = w0.shape[-1]
    rw, ra, rg = w2.shape[0], a2.shape[0], g2.shape[0]
    tb = _pick(seq, (256, 128, 64, 32, 16, 8))
    nb = seq // tb
    full = lambda shape: pl.BlockSpec(shape, lambda b, s: (0, 0))
    ospec = pl.BlockSpec((tb, width), lambda b, s: (b * nb + s, 0))
    return pl.pallas_call(
        functools.partial(_rwkv_prep_kernel, width=width, rw=rw, ra=ra),
        grid=(batch, nb),
        in_specs=[pl.BlockSpec((tb, cols), lambda b, s: (b * nb + s, 0)), full((1, cols)), full((1, width)),
                  full((rw, width)), full((1, width)), full((ra, width)), full((rg, width))],
        out_specs=[ospec] * 6,
        out_shape=[jax.ShapeDtypeStruct((t, width), F32)] * 6,
        scratch_shapes=[pltpu.VMEM((1, cols), F32)],
        compiler_params=_params("parallel", "arbitrary"),
        name="rwkv_prep",
    )(pd, mu.reshape(1, cols), w0.reshape(1, width), w2, a0.reshape(1, width), a2, g2)


def _rwkv_scan_kernel(r_ref, k_ref, v_ref, wl_ref, a_ref, g_ref, kk_ref, ka_ref, rk_ref, gw_ref, gb_ref,
                      o_ref, st_sc, *, heads, chunk, gn_eps):
    @pl.when(pl.program_id(2) == 0)
    def _():
        st_sc[...] = jnp.zeros(st_sc.shape, F32)

    n = chunk
    hw = heads * D_HEAD
    rows = heads * n
    ri = lax.broadcasted_iota(jnp.int32, (rows, hw), 0)
    ci = lax.broadcasted_iota(jnp.int32, (rows, hw), 1)
    same_head = (ri // n) == (ci // D_HEAD)
    rt_i = lax.broadcasted_iota(jnp.int32, (rows, rows), 0)
    ct_i = lax.broadcasted_iota(jnp.int32, (rows, rows), 1)
    strict = (rt_i % n) > (ct_i % n)
    incl = (rt_i % n) >= (ct_i % n)
    eye = (rt_i == ct_i).astype(F32)
    li = lax.broadcasted_iota(jnp.int32, (hw, hw), 0)
    lj = lax.broadcasted_iota(jnp.int32, (hw, hw), 1)
    head_ones = ((li // D_HEAD) == (lj // D_HEAD)).astype(BF16)
    tri_n = _tril(n).astype(F32)

    def to_blk(x):
        return jnp.where(same_head, jnp.concatenate([x] * heads, axis=0), 0.0).astype(BF16)

    def from_blk(y):
        out = y[0:n]
        for h in range(1, heads):
            out = out + y[h * n:(h + 1) * n]
        return out

    def head_sum(x):
        hi = x.astype(BF16)
        lo = (x - hi.astype(F32)).astype(BF16)
        s = jnp.dot(jnp.concatenate([hi, lo], axis=0), head_ones, preferred_element_type=F32)
        return s[:n] + s[n:]

    bdot = functools.partial(jnp.dot, preferred_element_type=F32)
    each = lambda f, *cols: [f(*a) for a in zip(*cols)]
    n_chunks, n_groups = r_ref.shape[0] // n, r_ref.shape[1] // hw
    ids = [(c, gi) for c in range(n_chunks) for gi in range(n_groups)]
    sls = [slice(c * n, (c + 1) * n) for c, _ in ids]
    lns = [slice(gi * hw, (gi + 1) * hw) for _, gi in ids]
    load = lambda ref: [ref[sl, ln] for sl, ln in zip(sls, lns)]
    par = lambda ref: [ref[:, ln] for ln in lns]
    r, k, v, wl, a_s = load(r_ref), load(k_ref), load(v_ref), load(wl_ref), load(a_ref)
    cum = each(lambda x: jnp.dot(tri_n, x, precision=HIGHEST, preferred_element_type=F32), wl)
    kk = each(lambda x, w: x * w, k, par(kk_ref))
    ksq = each(lambda x: head_sum(x * x), kk)
    kk = each(lambda x, s: x * lax.rsqrt(jnp.maximum(s, 1e-24)), kk, ksq)
    ke = each(lambda x, a, w: x * (1.0 + (a - 1.0) * w), k, a_s, par(ka_ref))
    cum_last = each(lambda x: x[n - 1:n, :], cum)
    e_neg = each(lambda x: jnp.exp(-x), cum)
    e_end = each(lambda x, xl: jnp.exp(xl - x), cum, cum_last)
    kka = each(lambda x, a: x * a, kk, a_s)
    lhs = each(lambda x, cu, w, rr: jnp.concatenate([to_blk(-x * jnp.exp(cu - w)), to_blk(rr * jnp.exp(cu))], axis=0),
               kk, cum, wl, r)
    rhs = each(lambda x, y, e: jnp.concatenate([to_blk(x * e), to_blk(y * e)], axis=0), kka, ke, e_neg)
    ends = each(lambda x, y, e: jnp.concatenate([to_blk(x * e), to_blk(y * e)], axis=0), kka, ke, e_end)
    vb = each(to_blk, v)
    cross = each(lambda a, b: lax.dot_general(a, b, NT_DIMS, preferred_element_type=F32), lhs, rhs)
    nab = each(lambda x: jnp.where(strict, x[:rows, :rows], 0.0), cross)
    aak = each(lambda x: jnp.where(strict, x[:rows, rows:], 0.0).astype(BF16), cross)
    rbk = each(lambda x: jnp.concatenate([jnp.where(incl, x[rows:, :rows], 0.0),
                                          jnp.where(incl, x[rows:, rows:], 0.0)], axis=1).astype(BF16), cross)
    inv = each(lambda x: eye + x, nab)
    pw = each(lambda x: x.astype(BF16), nab)
    for _ in range(max(1, int(math.ceil(math.log2(n))) - 1)):
        pw = each(lambda x: bdot(x, x).astype(BF16), pw)
        inv = each(lambda x, p: x + bdot(x.astype(BF16), p), inv, pw)
    inv = each(lambda x: x.astype(BF16), inv)
    akv = each(bdot, aak, vb)
    bonus = each(lambda rr, x, w, vv: head_sum(rr * x * w) * vv, r, ke, par(rk_ref), v)
    decay_end = each(jnp.exp, cum_last)
    ys = [None] * len(ids)
    for c in range(n_chunks):
        idx = [i for i, (cc, _) in enumerate(ids) if cc == c]
        take = lambda col: [col[i] for i in idx]
        s0 = [st_sc[ids[i][1]] for i in idx]
        ls = each(lambda a, s: lax.dot_general(a, s.astype(BF16), NT_DIMS, preferred_element_type=F32), take(lhs), s0)
        u = each(lambda m, x, y: bdot(m, (x[:rows] + y).astype(BF16)), take(inv), ls, take(akv))
        uv = each(lambda x, y: jnp.concatenate([x.astype(BF16), y], axis=0), u, take(vb))
        y = each(lambda x, m, z: from_blk(x[rows:] + bdot(m, z)), ls, take(rbk), uv)
        new = each(lambda s, d, z, e: s * d + lax.dot_general(z, e, TN_DIMS, preferred_element_type=F32),
                   s0, take(decay_end), uv, take(ends))
        for i, st, yy in zip(idx, new, y):
            st_sc[ids[i][1]] = st
            ys[i] = yy
    mu = each(lambda x: head_sum(x) * (1.0 / D_HEAD), ys)
    yc = each(lambda x, m: x - m, ys, mu)
    var = each(lambda x: head_sum(x * x) * (1.0 / D_HEAD), yc)
    yn = each(lambda x, s, w, b: x * lax.rsqrt(s + gn_eps) * w + b, yc, var, par(gw_ref), par(gb_ref))
    for sl, ln, x, b in zip(sls, lns, yn, bonus):
        o_ref[sl, ln] = ((x + b) * g_ref[sl, ln]).astype(o_ref.dtype)


def rwkv_scan(r, k, v, wl, a, g, k_k, k_a, r_k, gn_w, gn_b, batch, seq):
    t, width = r.shape
    hp = RWKV_HEADS_PER_STEP
    hw = hp * D_HEAD
    gw = _pick(width, (RWKV_GROUPS_PER_STEP * hw, hw))
    ch = RWKV_CHUNK
    tb = _pick(seq, (2 * ch, ch))
    nc = seq // tb
    xspec = pl.BlockSpec((tb, gw), lambda b, h, c: (b * nc + c, h))
    pspec = pl.BlockSpec((1, gw), lambda b, h, c: (0, h))
    vec = lambda p: p.reshape(1, width).astype(F32)
    return pl.pallas_call(
        functools.partial(_rwkv_scan_kernel, heads=hp, chunk=ch, gn_eps=D_GN_EPS),
        grid=(batch, width // gw, nc),
        in_specs=[xspec] * 6 + [pspec] * 5,
        out_specs=xspec,
        out_shape=jax.ShapeDtypeStruct((t, width), BF16),
        scratch_shapes=[pltpu.VMEM((gw // hw, hw, hw), F32)],
        compiler_params=_params("parallel", "parallel", "arbitrary"),
        name="rwkv_scan",
    )(r, k, v, wl, a, g, vec(k_k), vec(k_a), vec(r_k), vec(gn_w), vec(gn_b))


def _extract_top(work, dst_ref, count, want_rank=False):
    rank = jnp.full(work.shape, float(count), F32) if want_rank else None
    for r in range(count):
        m = jnp.max(work, axis=0, keepdims=True)
        dst_ref[r:r + 1, :] = m
        hit = work >= m
        if want_rank:
            rank = jnp.where(hit, float(r), rank)
        work = jnp.where(hit, -jnp.inf, work)
    return rank


def _peer_topk_kernel(q_ref, keys_ref, n1_ref, f1_ref, rk2_ref, e2_ref, a1_sc, a2_sc, cand_sc, top_sc):
    kk = P_TOPK
    for h in range(P_HEADS):
        sc = []
        for half in range(2):
            qh = q_ref[:, (2 * h + half) * LANES:(2 * h + half + 1) * LANES]
            sc.append(lax.dot_general(keys_ref[h, half], qh, NT_DIMS, precision=HIGHEST,
                                      preferred_element_type=F32))
        _extract_top(sc[0], a1_sc, kk)
        rank2 = _extract_top(sc[1], a2_sc, kk, want_rank=True)
        cand_sc[...] = jnp.full(cand_sc.shape, -jnp.inf, F32)
        off = 0
        for i in range(kk):
            nj = kk // (i + 1)
            cand_sc[off:off + nj, :] = a1_sc[i:i + 1, :] + a2_sc[0:nj, :]
            off += nj
        _extract_top(cand_sc[...], top_sc, kk)
        top = top_sc[...]
        best = top[0:1, :]
        tau = top[kk - 1:kk, :]
        zsum = jnp.sum(jnp.exp(top - best), axis=0, keepdims=True)
        count = jnp.zeros(sc[0].shape, F32)
        for j in range(kk):
            count = count + jnp.where(sc[0] + a2_sc[j:j + 1, :] >= tau, 1.0, 0.0)
        n1_ref[h] = count
        rk2_ref[h] = rank2.astype(BF16)
        f1_ref[h] = jnp.exp(sc[0] - a1_sc[0:1, :]) / zsum
        e2_ref[h] = jnp.exp(sc[1] - a2_sc[0:1, :]).astype(BF16)


def peer_topk(q, keys):
    t = q.shape[0]
    tt = _pick(t, (256, 128))
    n_cand = sum(P_TOPK // (i + 1) for i in range(P_TOPK))
    n_cand = -(-n_cand // 8) * 8
    big = pl.BlockSpec((P_HEADS, P_NKEYS, tt), lambda i: (0, 0, i))
    big_shape = jax.ShapeDtypeStruct((P_HEADS, P_NKEYS, t), F32)
    return pl.pallas_call(
        _peer_topk_kernel,
        grid=(t // tt,),
        in_specs=[pl.BlockSpec((tt, q.shape[1]), lambda i: (i, 0)),
                  pl.BlockSpec(keys.shape, lambda i: (0, 0, 0, 0))],
        out_specs=[big, big, big, big],
        out_shape=[big_shape, big_shape, big_shape.update(dtype=BF16), big_shape.update(dtype=BF16)],
        scratch_shapes=[pltpu.VMEM((P_TOPK, tt), F32), pltpu.VMEM((P_TOPK, tt), F32),
                        pltpu.VMEM((n_cand, tt), F32), pltpu.VMEM((P_TOPK, tt), F32)],
        compiler_params=_params("parallel"),
        name="peer_topk",
    )(q, keys)


def _peer_dense_kernel(sc_ref, x_ref, u_ref, v_ref, n1_ref, f1_ref, rk2_ref, e2_ref, o_ref, *, rows, chains):
    @pl.when(pl.program_id(1) == 0)
    def _():
        o_ref[...] = jnp.zeros(o_ref.shape, F32)

    tc = x_ref.shape[0] // chains
    hr = rows // 2
    dh = o_ref.shape[1] // 2
    tok = lambda ch: slice(ch * tc, (ch + 1) * tc)

    def activation(ch, half):
        es = slice(half * hr * P_NKEYS, (half + 1) * hr * P_NKEYS)
        return lax.dot_general(u_ref[es, :], x_ref[tok(ch), :], NT_DIMS, preferred_element_type=F32)

    def gated(ch, half, act):
        ts = tok(ch)
        sc = sc_ref[0]
        act = (act * (0.5 * sc)) * (1.0 + lax.erf(act * (sc * 2.0 ** -0.5)))
        act = act.astype(BF16)
        pk = P_NKEYS // BF16_ROWS
        parts = []
        for r in range(hr):
            row = half * hr + r
            coef = None
            for h in range(P_HEADS):
                n_row = jnp.broadcast_to(n1_ref[h, row:row + 1, ts], (BF16_ROWS, tc)).astype(BF16)
                f_row = jnp.broadcast_to(f1_ref[h, row:row + 1, ts], (BF16_ROWS, tc)).astype(BF16)
                sel = rk2_ref[h, :, :, ts] < n_row[None]
                c = jnp.where(sel, e2_ref[h, :, :, ts] * f_row[None], jnp.zeros((), BF16))
                coef = c if coef is None else coef + c
            a3 = act[r * P_NKEYS:(r + 1) * P_NKEYS, :].reshape(pk, BF16_ROWS, tc)
            parts.append((coef * a3).reshape(P_NKEYS, tc))
        return parts

    def project(ch, half, w):
        cs = slice(half * dh, (half + 1) * dh)
        o_ref[tok(ch), cs] += lax.dot_general(w, v_ref[:, cs], TN_DIMS, preferred_element_type=F32)

    acts = [activation(0, 0), activation(0, 1)]
    w_prev = None
    for ch in range(chains):
        nxt, parts = [], []
        for half in range(2):
            parts += gated(ch, half, acts[half])
            if ch + 1 < chains:
                nxt.append(activation(ch + 1, half))
            if w_prev is not None:
                project(ch - 1, half, w_prev)
        w_prev = jnp.concatenate(parts, axis=0)
        acts = nxt
    project(chains - 1, 0, w_prev)
    project(chains - 1, 1, w_prev)


def peer_dense(x, u, v, act_scale, n1, f1, rk2, e2):
    t, d = x.shape
    n = u.shape[0]
    tt = _pick(t, (1024, 512, 256, 128))
    chains = 2 if tt % 256 == 0 else 1
    rows = 8
    tn = rows * P_NKEYS
    n1r = n1.reshape(P_HEADS, P_NKEYS // rows, rows, t)
    f1r = f1.reshape(P_HEADS, P_NKEYS // rows, rows, t)
    pk = P_NKEYS // BF16_ROWS
    rk2 = rk2.reshape(P_HEADS, pk, BF16_ROWS, t)
    e2 = e2.reshape(P_HEADS, pk, BF16_ROWS, t)
    once = pl.Buffered(1)
    rowspec = pl.BlockSpec((P_HEADS, None, rows, tt), lambda i, e: (0, e, 0, i))
    fullspec = pl.BlockSpec((P_HEADS, pk, BF16_ROWS, tt), lambda i, e: (0, 0, 0, i), pipeline_mode=once)
    return pl.pallas_call(
        functools.partial(_peer_dense_kernel, rows=rows, chains=chains),
        grid=(t // tt, n // tn),
        in_specs=[pl.BlockSpec(memory_space=pltpu.SMEM),
                  pl.BlockSpec((tt, d), lambda i, e: (i, 0), pipeline_mode=once),
                  pl.BlockSpec((tn, d), lambda i, e: (e, 0)),
                  pl.BlockSpec((tn, d), lambda i, e: (e, 0)),
                  rowspec, rowspec, fullspec, fullspec],
        out_specs=pl.BlockSpec((tt, d), lambda i, e: (i, 0), pipeline_mode=once),
        out_shape=jax.ShapeDtypeStruct((t, d), F32),
        compiler_params=_params("parallel", "arbitrary"),
        name="peer_dense",
    )(act_scale.astype(F32), x, u, v, n1r, f1r, rk2, e2)


def _cast_kernel(sc_ref, x_ref, o_ref):
    o_ref[...] = (x_ref[...].astype(F32) * sc_ref[0]).astype(o_ref.dtype)


def scaled_cast(a, dtype, scale=None):
    r, c = a.shape
    tr = _pick(r, (512, 256, 128, 64, 32))
    scale = jnp.ones((1,), F32) if scale is None else scale.reshape(1).astype(F32)
    return pl.pallas_call(
        _cast_kernel,
        grid=(r // tr,),
        in_specs=[pl.BlockSpec(memory_space=pltpu.SMEM), pl.BlockSpec((tr, c), lambda i: (i, 0))],
        out_specs=pl.BlockSpec((tr, c), lambda i: (i, 0)),
        out_shape=jax.ShapeDtypeStruct((r, c), dtype),
        compiler_params=_params("parallel"),
        name="scaled_cast",
    )(scale, a)


def _absmax_kernel(x_ref, o_ref):
    m = jnp.max(jnp.abs(x_ref[...].astype(F32)), axis=0, keepdims=True)
    part = m[:, 0:LANES]
    for g in range(1, m.shape[1] // LANES):
        part = jnp.maximum(part, m[:, g * LANES:(g + 1) * LANES])
    o_ref[...] = jnp.broadcast_to(part, o_ref.shape)


def absmax(a):
    r, c = a.shape
    tr = _pick(r, (512, 256, 128, 64, 32))
    part = pl.pallas_call(
        _absmax_kernel,
        grid=(r // tr,),
        in_specs=[pl.BlockSpec((tr, c), lambda i: (i, 0))],
        out_specs=pl.BlockSpec((8, LANES), lambda i: (i, 0)),
        out_shape=jax.ShapeDtypeStruct((r // tr * 8, LANES), F32),
        compiler_params=_params("parallel"),
        name="absmax",
    )(a)
    return jnp.max(part)


def _fp8_scaled(a):
    k = jnp.floor(jnp.log2(FP8_TARGET_MAX / jnp.maximum(absmax(a), 1e-30)))
    scale = jnp.exp2(k)
    return scaled_cast(a, FP8, scale), 1.0 / scale


def peer_ffn(hn, w_q, keys, u, v):
    q = matmul([hn], [scaled_cast(w_q, BF16)], F32)
    n1, f1, rk2, e2 = peer_topk(q, keys)
    x8, inv_x = _fp8_scaled(hn)
    u8, inv_u = _fp8_scaled(u)
    return peer_dense(x8, u8, scaled_cast(v, BF16), (inv_x * inv_u).reshape(1), n1, f1, rk2, e2)


def even_mixer(hn, w_in, w_out, lam_params, subln_gain, lam_init, w_gate, b_gate, gla_norm, batch, seq, residual=None):
    na = w_out.shape[0] // 2
    a_heads = na // A_V_DIM
    b_heads = na // B_V_DIM
    sizes = (2 * a_heads * A_QK_DIM, 2 * a_heads * A_QK_DIM, na, b_heads * B_K_DIM, b_heads * B_K_DIM, na,
             w_gate.shape[0], na)
    names = ("aq", "ak", "av", "bq", "bk", "bv", "lo", "og")
    start, off = {}, 0
    for nm, sz in zip(names, sizes):
        start[nm] = off
        off += sz
    w_main = jnp.concatenate([w_in[:, :start["lo"]], w_in[:, start["og"]:]], axis=1).astype(BF16)
    cols = dict(start)
    cols["og"] = start["lo"]
    w_lo = _pad_cols(w_in[:, start["lo"]:start["og"]], LANES).astype(BF16)
    p = matmul([hn], [w_main], BF16)
    p_lo = matmul([hn], [w_lo], F32)
    qk = rope(p, rope_tables(seq, A_ROT), seq, width=start["av"], col_block=0, half=A_ROT // 2)
    lp = lam_params.astype(F32)
    lam = jnp.exp(jnp.sum(lp[0] * lp[1])) - jnp.exp(jnp.sum(lp[2] * lp[3])) + lam_init
    o_a = diff_attention(qk, p, lam, subln_gain, lam_init, batch, seq, start["av"])
    w_gate_p = jnp.pad(w_gate.astype(F32), ((0, LANES - w_gate.shape[0]), (0, 0)))
    o_b = gla(p, p_lo, w_gate_p, b_gate.astype(F32), gla_norm.astype(F32), batch, seq, cols)
    return matmul([o_a, o_b], scaled_cast(w_out, BF16), F32, residual, stacked=True)


def odd_mixer(hn, w_in, w_out, q_norm, kv_norm, w_uq, w_ukv, mu, w0, w2, a0, a2, g2,
              k_k, k_a, r_k, gn_w, gn_b, batch, seq, residual=None):
    q_rank, kv_rank = q_norm.shape[0], kv_norm.shape[0]
    width = w0.shape[0]
    rw, ra, rg = w2.shape[0], a2.shape[0], g2.shape[0]
    c_cols = q_rank + kv_rank + C_ROPE
    heads = w_uq.shape[1] // (C_NOPE + C_ROPE)
    c_pad = -(-(q_rank + kv_rank + LANES) // 256) * 256
    w_c = _pad_cols(w_in[:, :c_cols], c_pad).astype(BF16)
    pad_l = lambda m: -(-m // LANES) * LANES
    d0 = c_cols
    segs, mus, off = [], [], d0
    for sz in (width, width, width, rw, ra, rg):
        segs.append(_pad_cols(w_in[:, off:off + sz], pad_l(sz)))
        mus.append(jnp.pad(mu[off - d0:off - d0 + sz], (0, pad_l(sz) - sz)))
        off += sz
    w_d = jnp.concatenate(segs, axis=1).astype(BF16)
    mu_d = jnp.concatenate(mus).astype(F32)
    pad_r = lambda w: jnp.pad(w.astype(F32), ((0, pad_l(w.shape[0]) - w.shape[0]), (0, 0)))

    p_c = matmul([hn], [w_c], F32)
    p_d = matmul([hn], [w_d], F32)

    cq = add_rmsnorm([p_c], q_norm, BF16, width=q_rank, col_block=0)
    ckv = add_rmsnorm([p_c], kv_norm, BF16, width=kv_rank, col_block=q_rank // kv_rank)
    wq3 = w_uq.reshape(q_rank, heads, C_NOPE + C_ROPE)
    wq_nope = wq3[:, :, :C_NOPE].reshape(q_rank, heads * C_NOPE)
    wq_rope = jnp.pad(wq3[:, :, C_NOPE:], ((0, 0), (0, 0), (0, LANES - C_ROPE))).reshape(q_rank, heads * LANES)
    q_all = matmul([cq], [jnp.concatenate([wq_nope, wq_rope], axis=1).astype(BF16)], BF16)
    wkv3 = w_ukv.reshape(kv_rank, heads, C_NOPE + C_V)
    w_kv = jnp.concatenate([wkv3[:, :, :C_NOPE].reshape(kv_rank, heads * C_NOPE),
                            wkv3[:, :, C_NOPE:].reshape(kv_rank, heads * C_V)], axis=1).astype(BF16)
    kv = matmul([ckv], [w_kv], BF16)
    tables = rope_tables(seq, C_ROPE)
    q_rope = rope(q_all, tables, seq, width=heads * LANES, col_block=1, half=C_ROPE // 2)
    k_rope = rope(p_c, tables, seq, width=LANES, col_block=(q_rank + kv_rank) // LANES, half=C_ROPE // 2)
    o_c = mla_attention(q_all, q_rope, kv, k_rope, batch, seq)

    r, k, v, wl, a, g = rwkv_prep(p_d, mu_d, w0.astype(F32), pad_r(w2), a0.astype(F32), pad_r(a2), pad_r(g2),
                                  batch, seq)
    o_d = rwkv_scan(r, k, v, wl, a, g, k_k, k_a, r_k, gn_w, gn_b, batch, seq)
    return matmul([o_c, o_d], scaled_cast(w_out, BF16), F32, residual, stacked=True)


def kernel(x, norm_mix, norm_ffn, norm_final, even_w_in, even_w_out, diff_lambda, diff_subln, gla_w_gate, gla_b_gate, gla_norm, odd_w_in, odd_w_out, mla_q_norm, mla_kv_norm, mla_w_uq, mla_w_ukv, rwkv_mu, rwkv_w0, rwkv_w2, rwkv_a0, rwkv_a2, rwkv_g2, rwkv_k_k, rwkv_k_a, rwkv_r_k, rwkv_gn_w, rwkv_gn_b, peer_w_q, peer_keys, peer_u, peer_v):
    batch, seq, d = x.shape
    depth = norm_mix.shape[0]
    h = x.reshape(batch * seq, d)
    pending = []
    for layer in range(depth):
        j = layer // 2
        if pending:
            h, hn = add_rmsnorm([h] + pending, norm_mix[layer], BF16, emit_sum=True)
        else:
            hn = add_rmsnorm([h], norm_mix[layer], BF16)
        if layer % 2 == 0:
            lam_init = 0.8 - 0.6 * math.exp(-0.3 * layer)
            h = even_mixer(hn, even_w_in[j], even_w_out[j], diff_lambda[j], diff_subln[j], lam_init,
                           gla_w_gate[j], gla_b_gate[j], gla_norm[j], batch, seq, residual=h)
        else:
            h = odd_mixer(hn, odd_w_in[j], odd_w_out[j], mla_q_norm[j], mla_kv_norm[j], mla_w_uq[j],
                          mla_w_ukv[j], rwkv_mu[j], rwkv_w0[j], rwkv_w2[j], rwkv_a0[j], rwkv_a2[j],
                          rwkv_g2[j], rwkv_k_k[j], rwkv_k_a[j], rwkv_r_k[j], rwkv_gn_w[j], rwkv_gn_b[j],
                          batch, seq, residual=h)
        hn2 = add_rmsnorm([h], norm_ffn[layer], BF16)
        pending = [peer_ffn(hn2, peer_w_q[layer], peer_keys[layer], peer_u[layer], peer_v[layer])]
    out = add_rmsnorm([h] + pending, norm_final, F32)
    return out.reshape(batch, seq, d).astype(x.dtype)
```

```python
import functools
import math

import jax
import jax.numpy as jnp
from jax import lax
from jax.experimental import pallas as pl
from jax.experimental.pallas import tpu as pltpu

F32 = jnp.float32
BF16 = jnp.bfloat16
FP8 = jnp.float8_e4m3fn
FP8_TARGET_MAX = 256.0
HIGHEST = lax.Precision.HIGHEST

LANES = 128
BF16_ROWS = 16
VMEM_LIMIT = 56 * 1024 * 1024

EPS = 1e-6
ROPE_THETA = 500000.0
A_QK_DIM = 128
A_V_DIM = 256
A_ROT = A_QK_DIM // 4
B_V_DIM = 512
B_K_DIM = 256
B_GATE_TAU = 16.0
GLA_CHUNK = 64
C_NOPE = 128
C_ROPE = 64
C_V = 128
D_HEAD = 64
D_GN_EPS = 64e-5
RWKV_CHUNK = 64
ATTN_BLOCKS = (1024, 512, 256, 128)
ATTN_SUB = 4
RWKV_HEADS_PER_STEP = 4
RWKV_GROUPS_PER_STEP = 4
P_HEADS = 8
P_NKEYS = 128
P_TOPK = 16

NT_DIMS = (((1,), (1,)), ((), ()))
TN_DIMS = (((0,), (0,)), ((), ()))


def _params(*sem):
    return pltpu.CompilerParams(dimension_semantics=sem, vmem_limit_bytes=VMEM_LIMIT)


def _pick(n, cands):
    for c in cands:
        if n % c == 0:
            return c
    raise ValueError(f"no tile in {cands} divides {n}")


def _pad_cols(w, n):
    return jnp.pad(w, ((0, 0), (0, n - w.shape[1])))


def _norm_kernel(*refs, eps, n_add, emit_sum, emit_fp8):
    xs = refs[:n_add]
    g_ref = refs[n_add]
    outs = list(refs[n_add + 1 + emit_fp8:])
    x = xs[0][...].astype(F32)
    for r in xs[1:]:
        x = x + r[...].astype(F32)
    y = x * lax.rsqrt(jnp.mean(x * x, axis=-1, keepdims=True) + eps) * g_ref[...]
    if emit_sum:
        outs.pop(0)[...] = x
    o_ref = outs.pop(0)
    o_ref[...] = y.astype(o_ref.dtype)
    if emit_fp8:
        outs.pop(0)[...] = (y * refs[n_add + 1][0]).astype(FP8)


def add_rmsnorm(xs, g, out_dtype, *, emit_sum=False, fp8_scale=None, width=None, col_block=0, eps=EPS):
    t = xs[0].shape[0]
    width = width or xs[0].shape[1]
    emit_fp8 = fp8_scale is not None
    tr = _pick(t, (128, 64, 32, 16, 8))
    spec = pl.BlockSpec((tr, width), lambda i: (i, col_block))
    ospec = pl.BlockSpec((tr, width), lambda i: (i, 0))
    out_shape = [jax.ShapeDtypeStruct((t, width), out_dtype)]
    out_specs = [ospec]
    if emit_sum:
        out_shape = [jax.ShapeDtypeStruct((t, width), F32)] + out_shape
        out_specs = [ospec, ospec]
    extra_in, extra_specs = [], []
    if emit_fp8:
        out_shape = out_shape + [jax.ShapeDtypeStruct((t, width), FP8)]
        out_specs = out_specs + [ospec]
        extra_in = [fp8_scale.reshape(1).astype(F32)]
        extra_specs = [pl.BlockSpec(memory_space=pltpu.SMEM)]
    res = pl.pallas_call(
        functools.partial(_norm_kernel, eps=eps, n_add=len(xs), emit_sum=emit_sum, emit_fp8=emit_fp8),
        grid=(t // tr,),
        in_specs=[spec] * len(xs) + [pl.BlockSpec((1, width), lambda i: (0, 0))] + extra_specs,
        out_specs=out_specs,
        out_shape=out_shape,
        compiler_params=_params("parallel"),
        name="add_rmsnorm",
    )(*xs, g.reshape(1, width).astype(F32), *extra_in)
    return res if len(res) > 1 else res[0]


def _mm_kernel(*refs, n_in, has_residual):
    o_ref = refs[-1]
    acc = jnp.dot(refs[0][...], refs[n_in][...], preferred_element_type=F32)
    for i in range(1, n_in):
        acc = acc + jnp.dot(refs[i][...], refs[n_in + i][...], preferred_element_type=F32)
    if has_residual:
        acc = acc + refs[2 * n_in][...]
    o_ref[...] = acc.astype(o_ref.dtype)


def matmul(xs, ws, out_dtype, residual=None, stacked=False):
    m = xs[0].shape[0]
    if stacked:
        kw = xs[0].shape[1]
        ws = [ws] * len(xs)
    n = ws[0].shape[1]
    tm = _pick(m, (1024, 512, 256, 128, 64, 32, 16))
    tn = _pick(n, (512, 768, 640, 256, 128))
    in_specs = [pl.BlockSpec((tm, x.shape[1]), lambda i, j: (i, 0)) for x in xs]
    if stacked:
        in_specs += [pl.BlockSpec((kw, tn), functools.partial(lambda i, j, b: (b, j), b=b)) for b in range(len(xs))]
    else:
        in_specs += [pl.BlockSpec((w.shape[0], tn), lambda i, j: (0, j)) for w in ws]
    extra = []
    if residual is not None:
        in_specs.append(pl.BlockSpec((tm, tn), lambda i, j: (i, j)))
        extra = [residual]
    return pl.pallas_call(
        functools.partial(_mm_kernel, n_in=len(xs), has_residual=residual is not None),
        grid=(m // tm, n // tn),
        in_specs=in_specs,
        out_specs=pl.BlockSpec((tm, tn), lambda i, j: (i, j)),
        out_shape=jax.ShapeDtypeStruct((m, n), out_dtype),
        compiler_params=_params("parallel", "arbitrary"),
        name="matmul",
    )(*xs, *ws, *extra)


def rope_tables(seq, rot_dim):
    half = rot_dim // 2
    inv = 1.0 / (ROPE_THETA ** (jnp.arange(0, rot_dim, 2, dtype=F32) / rot_dim))
    ang = jnp.arange(seq, dtype=F32)[:, None] * inv[None, :]
    cos, sin = jnp.cos(ang), jnp.sin(ang)
    z = jnp.zeros((seq, LANES - 2 * half), F32)
    zh = jnp.zeros((seq, half), F32)
    c = jnp.concatenate([cos, cos, jnp.ones_like(z)], axis=1)
    s_up = jnp.concatenate([zh, sin, z], axis=1)
    s_dn = jnp.concatenate([-sin, zh, z], axis=1)
    return c, s_up, s_dn


def _rope_kernel(x_ref, c_ref, su_ref, sd_ref, o_ref, *, half, groups):
    c, su, sd = c_ref[...], su_ref[...], sd_ref[...]
    for g in range(groups):
        sl = slice(g * LANES, (g + 1) * LANES)
        x = x_ref[:, sl].astype(F32)
        y = x * c + pltpu.roll(x, half, 1) * su + pltpu.roll(x, LANES - half, 1) * sd
        o_ref[:, sl] = y.astype(o_ref.dtype)


def rope(x, tables, seq, *, width, col_block, half):
    t = x.shape[0]
    tr = _pick(seq, (256, 128, 64, 32, 16))
    nb = seq // tr
    tspec = pl.BlockSpec((tr, LANES), lambda i: (i % nb, 0))
    return pl.pallas_call(
        functools.partial(_rope_kernel, half=half, groups=width // LANES),
        grid=(t // tr,),
        in_specs=[pl.BlockSpec((tr, width), lambda i: (i, col_block)), tspec, tspec, tspec],
        out_specs=pl.BlockSpec((tr, width), lambda i: (i, 0)),
        out_shape=jax.ShapeDtypeStruct((t, width), BF16),
        compiler_params=_params("parallel"),
        name="rope",
    )(x, *tables)


LOG2E = 1.4426950408889634


def _scaled_q(q, scale):
    return (q.astype(F32) * (scale * LOG2E)).astype(BF16)


def _softmax_chains(chains):
    s = [lax.dot_general(q, k, NT_DIMS, preferred_element_type=F32) for q, k, *_ in chains]
    def masked(x, tail):
        if tail is None:
            return x
        w = tail.shape[1]
        last = jnp.where(tail, x[:, x.shape[1] - w:], -jnp.inf)
        return last if w == x.shape[1] else jnp.concatenate([x[:, :x.shape[1] - w], last], axis=1)

    s = [masked(x, c[6]) for x, c in zip(s, chains)]
    m_prev = [c[3][...] for c in chains]
    m_new = [jnp.maximum(mp, jnp.max(x, axis=-1, keepdims=True)) for mp, x in zip(m_prev, s)]
    alpha = [jnp.exp2(mp - mn) for mp, mn in zip(m_prev, m_new)]
    p = [jnp.exp2(x - mn) for x, mn in zip(s, m_new)]
    pv = [jnp.dot(x.astype(c[2].dtype), c[2], preferred_element_type=F32) for x, c in zip(p, chains)]
    for c, a, x, y, mn in zip(chains, alpha, p, pv, m_new):
        c[4][...] = a * c[4][...] + jnp.sum(x, axis=-1, keepdims=True)
        c[5][...] = a * c[5][...] + y
        c[3][...] = mn


def _causal_sweep(i, tq, step):
    def body(j, carry):
        step(pl.ds(pl.multiple_of(j * tq, tq), tq), False)
        return carry
    lax.fori_loop(0, i, body, 0)
    step(pl.ds(pl.multiple_of(i * tq, tq), tq), True)


def _row_groups(q, k, v, m_sc, l_sc, acc_sc, diagonal):
    rs = q.shape[0] // ATTN_SUB
    chains = []
    tri = _tril(rs)
    for g in range(ATTN_SUB):
        qg = q[g * rs:(g + 1) * rs]
        if diagonal:
            r = g * rs + lax.broadcasted_iota(jnp.int32, (rs, k.shape[0]), 0)
            c = lax.broadcasted_iota(jnp.int32, (rs, k.shape[0]), 1)
            chains.append((qg, k, v, m_sc.at[g], l_sc.at[g], acc_sc.at[g], c <= r))
        else:
            chains.append((qg, k, v, m_sc.at[g], l_sc.at[g], acc_sc.at[g], None))
    return chains


def _diff_attn_kernel(lam_ref, q_ref, k_ref, v_ref, g_ref, o_ref, m_sc, l_sc, acc_sc, *, scale, tq, out_scale, eps):
    i = pl.program_id(2)
    m_sc[...] = jnp.full(m_sc.shape, -jnp.inf, F32)
    l_sc[...] = jnp.zeros(l_sc.shape, F32)
    acc_sc[...] = jnp.zeros(acc_sc.shape, F32)
    qs = [_scaled_q(q_ref[:, mp * A_QK_DIM:(mp + 1) * A_QK_DIM], scale) for mp in range(2)]

    def step(rows, diagonal):
        v = v_ref[rows, :]
        chains = []
        for mp in range(2):
            k = k_ref[rows, mp * A_QK_DIM:(mp + 1) * A_QK_DIM]
            chains += _row_groups(qs[mp], k, v, m_sc.at[mp], l_sc.at[mp], acc_sc.at[mp], diagonal)
        _softmax_chains(chains)

    _causal_sweep(i, tq, step)
    rs = tq // ATTN_SUB
    for g in range(ATTN_SUB):
        o = acc_sc[0, g] / l_sc[0, g] - lam_ref[0] * (acc_sc[1, g] / l_sc[1, g])
        o = o * lax.rsqrt(jnp.mean(o * o, axis=-1, keepdims=True) + eps) * g_ref[...] * out_scale
        o_ref[g * rs:(g + 1) * rs, :] = o.astype(o_ref.dtype)


def diff_attention(qk, p, lam, subln_gain, lam_init, batch, seq, v_col):
    t = qk.shape[0]
    heads = qk.shape[1] // (4 * A_QK_DIM)
    hw = 2 * A_QK_DIM
    tq = _pick(seq, ATTN_BLOCKS)
    rs = tq // ATTN_SUB
    nq = seq // tq
    vb = v_col // A_V_DIM
    kern = functools.partial(_diff_attn_kernel, scale=A_QK_DIM ** -0.5, tq=tq, out_scale=1.0 - lam_init, eps=EPS)
    return pl.pallas_call(
        kern,
        grid=(batch, heads, nq),
        in_specs=[
            pl.BlockSpec(memory_space=pltpu.SMEM),
            pl.BlockSpec((tq, hw), lambda b, h, i: (b * nq + i, h)),
            pl.BlockSpec((seq, hw), lambda b, h, i: (b, heads + h)),
            pl.BlockSpec((seq, A_V_DIM), lambda b, h, i: (b, vb + h)),
            pl.BlockSpec((1, A_V_DIM), lambda b, h, i: (0, 0)),
        ],
        out_specs=pl.BlockSpec((tq, A_V_DIM), lambda b, h, i: (b * nq + i, h)),
        out_shape=jax.ShapeDtypeStruct((t, heads * A_V_DIM), BF16),
        scratch_shapes=[pltpu.VMEM((2, ATTN_SUB, rs, 1), F32), pltpu.VMEM((2, ATTN_SUB, rs, 1), F32),
                        pltpu.VMEM((2, ATTN_SUB, rs, A_V_DIM), F32)],
        compiler_params=_params("parallel", "parallel", "arbitrary"),
        name="diff_attention",
    )(lam.reshape(1).astype(F32), qk, qk, p, subln_gain.reshape(1, A_V_DIM).astype(F32))


def _mla_attn_kernel(qn_ref, qr_ref, kn_ref, kr_ref, v_ref, o_ref, m_sc, l_sc, acc_sc, *, scale, tq):
    i = pl.program_id(2)
    m_sc[...] = jnp.full(m_sc.shape, -jnp.inf, F32)
    l_sc[...] = jnp.zeros(l_sc.shape, F32)
    acc_sc[...] = jnp.zeros(acc_sc.shape, F32)
    q = _scaled_q(jnp.concatenate([qn_ref[...], qr_ref[...]], axis=-1), scale)

    def step(rows, diagonal):
        k = jnp.concatenate([kn_ref[rows, :], kr_ref[rows, :]], axis=-1)
        _softmax_chains(_row_groups(q, k, v_ref[rows, :], m_sc, l_sc, acc_sc, diagonal))

    _causal_sweep(i, tq, step)
    rs = tq // ATTN_SUB
    for g in range(ATTN_SUB):
        o_ref[g * rs:(g + 1) * rs, :] = (acc_sc[g] / l_sc[g]).astype(o_ref.dtype)


def mla_attention(q_all, q_rope, kv, k_rope, batch, seq):
    t = kv.shape[0]
    heads = kv.shape[1] // (C_NOPE + C_V)
    tq = _pick(seq, ATTN_BLOCKS)
    rs = tq // ATTN_SUB
    nq = seq // tq
    qmap = lambda b, h, i: (b * nq + i, h)
    return pl.pallas_call(
        functools.partial(_mla_attn_kernel, scale=(C_NOPE + C_ROPE) ** -0.5, tq=tq),
        grid=(batch, heads, nq),
        in_specs=[
            pl.BlockSpec((tq, C_NOPE), qmap),
            pl.BlockSpec((tq, LANES), qmap),
            pl.BlockSpec((seq, C_NOPE), lambda b, h, i: (b, h)),
            pl.BlockSpec((seq, LANES), lambda b, h, i: (b, 0)),
            pl.BlockSpec((seq, C_V), lambda b, h, i: (b, heads + h)),
        ],
        out_specs=pl.BlockSpec((tq, C_V), qmap),
        out_shape=jax.ShapeDtypeStruct((t, heads * C_V), BF16),
        scratch_shapes=[pltpu.VMEM((ATTN_SUB, rs, 1), F32), pltpu.VMEM((ATTN_SUB, rs, 1), F32),
                        pltpu.VMEM((ATTN_SUB, rs, C_V), F32)],
        compiler_params=_params("parallel", "parallel", "arbitrary"),
        name="mla_attention",
    )(q_all, q_rope, kv, k_rope, kv)


def _tril(n, strict=False):
    r = lax.broadcasted_iota(jnp.int32, (n, n), 0)
    c = lax.broadcasted_iota(jnp.int32, (n, n), 1)
    return (r > c) if strict else (r >= c)


def _gla_kernel(q_ref, k_ref, v_ref, og_ref, lo_ref, wg_ref, bg_ref, gn_ref, o_ref, st_sc, *, scale, eps, chunk, heads):
    @pl.when(pl.program_id(1) == 0)
    def _():
        st_sc[...] = jnp.zeros(st_sc.shape, F32)

    n = chunk
    n_chunks = q_ref.shape[0] // n
    dk, dv = B_K_DIM, B_V_DIM
    each = lambda f, *cols: [f(*a) for a in zip(*cols)]
    tri = _tril(n)
    trif = tri.astype(F32)
    rows = [slice(c * n, (c + 1) * n) for c in range(n_chunks)]
    z = [jnp.dot(lo_ref[r, :], wg_ref[...], precision=HIGHEST, preferred_element_type=F32) + bg_ref[...] for r in rows]
    log_a = each(lambda x: (jnp.minimum(x, 0.0) - jnp.log1p(jnp.exp(-jnp.abs(x)))) / B_GATE_TAU, z)
    b_all = each(lambda x: jnp.dot(trif, x, precision=HIGHEST, preferred_element_type=F32), log_a)
    ids = [(c, h) for c in range(n_chunks) for h in range(heads)]
    kl = [slice(h * dk, (h + 1) * dk) for _, h in ids]
    vl = [slice(h * dv, (h + 1) * dv) for _, h in ids]
    rw = [rows[c] for c, _ in ids]
    b = [b_all[c][:, s] for (c, _), s in zip(ids, kl)]
    b_last = each(lambda x: x[n - 1:n, :], b)
    q = [q_ref[r, s].astype(F32) * scale for r, s in zip(rw, kl)]
    k = [k_ref[r, s].astype(F32) for r, s in zip(rw, kl)]
    v = [v_ref[r, s] for r, s in zip(rw, vl)]
    qe = each(lambda x, bb: (x * jnp.exp(bb)).astype(BF16), q, b)
    ke = each(lambda x, bb: (x * jnp.exp(-bb)).astype(BF16), k, b)
    kend = each(lambda x, bb, bl: (x * jnp.exp(bl - bb)).astype(BF16), k, b, b_last)
    attn = each(lambda a, c: jnp.where(tri, lax.dot_general(a, c, NT_DIMS, preferred_element_type=F32), 0.0).astype(BF16),
                qe, ke)
    o_intra = each(lambda a, vv: jnp.dot(a, vv, preferred_element_type=F32), attn, v)
    decay = each(jnp.exp, b_last)
    outs = [None] * len(ids)
    for c in range(n_chunks):
        idx = [i for i, (cc, _) in enumerate(ids) if cc == c]
        st = [st_sc[ids[i][1]] for i in idx]
        o_inter = [lax.dot_general(qe[i], s.astype(BF16), NT_DIMS, preferred_element_type=F32) for i, s in zip(idx, st)]
        new = [s * decay[i] + lax.dot_general(v[i], kend[i], TN_DIMS, preferred_element_type=F32)
               for i, s in zip(idx, st)]
        for i, s_new, oi in zip(idx, new, o_inter):
            st_sc[ids[i][1]] = s_new
            outs[i] = oi + o_intra[i]
    for r, s, o in zip(rw, vl, outs):
        o = o * lax.rsqrt(jnp.mean(o * o, axis=-1, keepdims=True) + eps) * gn_ref[...]
        og = og_ref[r, s].astype(F32)
        o_ref[r, s] = (o * (og * jax.nn.sigmoid(og))).astype(o_ref.dtype)


def gla(p, p_lo, w_gate, b_gate, gn, batch, seq, cols):
    t = p.shape[0]
    heads = w_gate.shape[1] // B_K_DIM
    ch = GLA_CHUNK
    tb = _pick(seq, (2 * ch, ch))
    nb = seq // tb
    kw, vw = heads * B_K_DIM, heads * B_V_DIM
    row = lambda b, c: b * nb + c
    return pl.pallas_call(
        functools.partial(_gla_kernel, scale=B_K_DIM ** -0.5, eps=EPS, chunk=ch, heads=heads),
        grid=(batch, nb),
        in_specs=[
            pl.BlockSpec((tb, kw), lambda b, c: (row(b, c), cols["bq"] // kw)),
            pl.BlockSpec((tb, kw), lambda b, c: (row(b, c), cols["bk"] // kw)),
            pl.BlockSpec((tb, vw), lambda b, c: (row(b, c), cols["bv"] // vw)),
            pl.BlockSpec((tb, vw), lambda b, c: (row(b, c), cols["og"] // vw)),
            pl.BlockSpec((tb, LANES), lambda b, c: (row(b, c), 0)),
            pl.BlockSpec((LANES, kw), lambda b, c: (0, 0)),
            pl.BlockSpec((1, kw), lambda b, c: (0, 0)),
            pl.BlockSpec((1, B_V_DIM), lambda b, c: (0, 0)),
        ],
        out_specs=pl.BlockSpec((tb, vw), lambda b, c: (row(b, c), 0)),
        out_shape=jax.ShapeDtypeStruct((t, vw), BF16),
        scratch_shapes=[pltpu.VMEM((heads, B_V_DIM, B_K_DIM), F32)],
        compiler_params=_params("parallel", "arbitrary"),
        name="gla",
    )(p, p, p, p, p_lo, w_gate, b_gate.reshape(1, -1), gn.reshape(1, -1))


def _rwkv_prep_kernel(pd_ref, mu_ref, w0_ref, w2_ref, a0_ref, a2_ref, g2_ref,
                      r_ref, k_ref, v_ref, wl_ref, a_ref, g_ref, carry_sc, *, width, rw, ra):
    @pl.when(pl.program_id(1) == 0)
    def _():
        carry_sc[...] = jnp.zeros(carry_sc.shape, F32)

    x = pd_ref[...]
    n = x.shape[0]
    first = lax.broadcasted_iota(jnp.int32, x.shape, 0) == 0
    prev = jnp.where(first, carry_sc[...], pltpu.roll(x, 1, 0))
    carry_sc[...] = x[n - 1:n, :]
    xs = x + (prev - x) * mu_ref[...]
    r_ref[...] = xs[:, 0:width]
    k_ref[...] = xs[:, width:2 * width]
    v_ref[...] = xs[:, 2 * width:3 * width]
    c0 = 3 * width
    xw, xa, xg = xs[:, c0:c0 + rw], xs[:, c0 + rw:c0 + rw + ra], xs[:, c0 + rw + ra:]
    zw = w0_ref[...] + jnp.dot(jnp.tanh(xw), w2_ref[...], precision=HIGHEST, preferred_element_type=F32)
    w = -(jnp.maximum(-zw, 0.0) + jnp.log1p(jnp.exp(-jnp.abs(zw)))) - 0.5
    wl_ref[...] = -jnp.exp(w)
    za = a0_ref[...] + jnp.dot(xa, a2_ref[...], precision=HIGHEST, preferred_element_type=F32)
    a_ref[...] = jax.nn.sigmoid(za)
    g_ref[...] = jnp.dot(jax.nn.sigmoid(xg), g2_ref[...], precision=HIGHEST, preferred_element_type=F32)


def rwkv_prep(pd, mu, w0, w2, a0, a2, g2, batch, seq):
    t, cols = pd.shape
    width = w0.shape[-1]
    rw, ra, rg = w2.shape[0], a2.shape[0], g2.shape[0]
    tb = _pick(seq, (256, 128, 64, 32, 16, 8))
    nb = seq // tb
    full = lambda shape: pl.BlockSpec(shape, lambda b, s: (0, 0))
    ospec = pl.BlockSpec((tb, width), lambda b, s: (b * nb + s, 0))
    return pl.pallas_call(
        functools.partial(_rwkv_prep_kernel, width=width, rw=rw, ra=ra),
        grid=(batch, nb),
        in_specs=[pl.BlockSpec((tb, cols), lambda b, s: (b * nb + s, 0)), full((1, cols)), full((1, width)),
                  full((rw, width)), full((1, width)), full((ra, width)), full((rg, width))],
        out_specs=[ospec] * 6,
        out_shape=[jax.ShapeDtypeStruct((t, width), F32)] * 6,
        scratch_shapes=[pltpu.VMEM((1, cols), F32)],
        compiler_params=_params("parallel", "arbitrary"),
        name="rwkv_prep",
    )(pd, mu.reshape(1, cols), w0.reshape(1, width), w2, a0.reshape(1, width), a2, g2)


def _rwkv_scan_kernel(r_ref, k_ref, v_ref, wl_ref, a_ref, g_ref, kk_ref, ka_ref, rk_ref, gw_ref, gb_ref,
                      o_ref, st_sc, *, heads, chunk, gn_eps):
    @pl.when(pl.program_id(2) == 0)
    def _():
        st_sc[...] = jnp.zeros(st_sc.shape, F32)

    n = chunk
    hw = heads * D_HEAD
    rows = heads * n
    ri = lax.broadcasted_iota(jnp.int32, (rows, hw), 0)
    ci = lax.broadcasted_iota(jnp.int32, (rows, hw), 1)
    same_head = (ri // n) == (ci // D_HEAD)
    rt_i = lax.broadcasted_iota(jnp.int32, (rows, rows), 0)
    ct_i = lax.broadcasted_iota(jnp.int32, (rows, rows), 1)
    strict = (rt_i % n) > (ct_i % n)
    incl = (rt_i % n) >= (ct_i % n)
    eye = (rt_i == ct_i).astype(F32)
    li = lax.broadcasted_iota(jnp.int32, (hw, hw), 0)
    lj = lax.broadcasted_iota(jnp.int32, (hw, hw), 1)
    head_ones = ((li // D_HEAD) == (lj // D_HEAD)).astype(BF16)
    tri_n = _tril(n).astype(F32)

    def to_blk(x):
        return jnp.where(same_head, jnp.concatenate([x] * heads, axis=0), 0.0).astype(BF16)

    def from_blk(y):
        out = y[0:n]
        for h in range(1, heads):
            out = out + y[h * n:(h + 1) * n]
        return out

    def head_sum(x):
        hi = x.astype(BF16)
        lo = (x - hi.astype(F32)).astype(BF16)
        s = jnp.dot(jnp.concatenate([hi, lo], axis=0), head_ones, preferred_element_type=F32)
        return s[:n] + s[n:]

    bdot = functools.partial(jnp.dot, preferred_element_type=F32)
    each = lambda f, *cols: [f(*a) for a in zip(*cols)]
    n_chunks, n_groups = r_ref.shape[0] // n, r_ref.shape[1] // hw
    ids = [(c, gi) for c in range(n_chunks) for gi in range(n_groups)]
    sls = [slice(c * n, (c + 1) * n) for c, _ in ids]
    lns = [slice(gi * hw, (gi + 1) * hw) for _, gi in ids]
    load = lambda ref: [ref[sl, ln] for sl, ln in zip(sls, lns)]
    par = lambda ref: [ref[:, ln] for ln in lns]
    r, k, v, wl, a_s = load(r_ref), load(k_ref), load(v_ref), load(wl_ref), load(a_ref)
    cum = each(lambda x: jnp.dot(tri_n, x, precision=HIGHEST, preferred_element_type=F32), wl)
    kk = each(lambda x, w: x * w, k, par(kk_ref))
    ksq = each(lambda x: head_sum(x * x), kk)
    kk = each(lambda x, s: x * lax.rsqrt(jnp.maximum(s, 1e-24)), kk, ksq)
    ke = each(lambda x, a, w: x * (1.0 + (a - 1.0) * w), k, a_s, par(ka_ref))
    cum_last = each(lambda x: x[n - 1:n, :], cum)
    e_neg = each(lambda x: jnp.exp(-x), cum)
    e_end = each(lambda x, xl: jnp.exp(xl - x), cum, cum_last)
    kka = each(lambda x, a: x * a, kk, a_s)
    lhs = each(lambda x, cu, w, rr: jnp.concatenate([to_blk(-x * jnp.exp(cu - w)), to_blk(rr * jnp.exp(cu))], axis=0),
               kk, cum, wl, r)
    rhs = each(lambda x, y, e: jnp.concatenate([to_blk(x * e), to_blk(y * e)], axis=0), kka, ke, e_neg)
    ends = each(lambda x, y, e: jnp.concatenate([to_blk(x * e), to_blk(y * e)], axis=0), kka, ke, e_end)
    vb = each(to_blk, v)
    cross = each(lambda a, b: lax.dot_general(a, b, NT_DIMS, preferred_element_type=F32), lhs, rhs)
    nab = each(lambda x: jnp.where(strict, x[:rows, :rows], 0.0), cross)
    aak = each(lambda x: jnp.where(strict, x[:rows, rows:], 0.0).astype(BF16), cross)
    rbk = each(lambda x: jnp.concatenate([jnp.where(incl, x[rows:, :rows], 0.0),
                                          jnp.where(incl, x[rows:, rows:], 0.0)], axis=1).astype(BF16), cross)
    inv = each(lambda x: eye + x, nab)
    pw = each(lambda x: x.astype(BF16), nab)
    for _ in range(max(1, int(math.ceil(math.log2(n))) - 1)):
        pw = each(lambda x: bdot(x, x).astype(BF16), pw)
        inv = each(lambda x, p: x + bdot(x.astype(BF16), p), inv, pw)
    inv = each(lambda x: x.astype(BF16), inv)
    akv = each(bdot, aak, vb)
    bonus = each(lambda rr, x, w, vv: head_sum(rr * x * w) * vv, r, ke, par(rk_ref), v)
    decay_end = each(jnp.exp, cum_last)
    ys = [None] * len(ids)
    for c in range(n_chunks):
        idx = [i for i, (cc, _) in enumerate(ids) if cc == c]
        take = lambda col: [col[i] for i in idx]
        s0 = [st_sc[ids[i][1]] for i in idx]
        ls = each(lambda a, s: lax.dot_general(a, s.astype(BF16), NT_DIMS, preferred_element_type=F32), take(lhs), s0)
        u = each(lambda m, x, y: bdot(m, (x[:rows] + y).astype(BF16)), take(inv), ls, take(akv))
        uv = each(lambda x, y: jnp.concatenate([x.astype(BF16), y], axis=0), u, take(vb))
        y = each(lambda x, m, z: from_blk(x[rows:] + bdot(m, z)), ls, take(rbk), uv)
        new = each(lambda s, d, z, e: s * d + lax.dot_general(z, e, TN_DIMS, preferred_element_type=F32),
                   s0, take(decay_end), uv, take(ends))
        for i, st, yy in zip(idx, new, y):
            st_sc[ids[i][1]] = st
            ys[i] = yy
    mu = each(lambda x: head_sum(x) * (1.0 / D_HEAD), ys)
    yc = each(lambda x, m: x - m, ys, mu)
    var = each(lambda x: head_sum(x * x) * (1.0 / D_HEAD), yc)
    yn = each(lambda x, s, w, b: x * lax.rsqrt(s + gn_eps) * w + b, yc, var, par(gw_ref), par(gb_ref))
    for sl, ln, x, b in zip(sls, lns, yn, bonus):
        o_ref[sl, ln] = ((x + b) * g_ref[sl, ln]).astype(o_ref.dtype)


def rwkv_scan(r, k, v, wl, a, g, k_k, k_a, r_k, gn_w, gn_b, batch, seq):
    t, width = r.shape
    hp = RWKV_HEADS_PER_STEP
    hw = hp * D_HEAD
    gw = _pick(width, (RWKV_GROUPS_PER_STEP * hw, hw))
    ch = RWKV_CHUNK
    tb = _pick(seq, (2 * ch, ch))
    nc = seq // tb
    xspec = pl.BlockSpec((tb, gw), lambda b, h, c: (b * nc + c, h))
    pspec = pl.BlockSpec((1, gw), lambda b, h, c: (0, h))
    vec = lambda p: p.reshape(1, width).astype(F32)
    return pl.pallas_call(
        functools.partial(_rwkv_scan_kernel, heads=hp, chunk=ch, gn_eps=D_GN_EPS),
        grid=(batch, width // gw, nc),
        in_specs=[xspec] * 6 + [pspec] * 5,
        out_specs=xspec,
        out_shape=jax.ShapeDtypeStruct((t, width), BF16),
        scratch_shapes=[pltpu.VMEM((gw // hw, hw, hw), F32)],
        compiler_params=_params("parallel", "parallel", "arbitrary"),
        name="rwkv_scan",
    )(r, k, v, wl, a, g, vec(k_k), vec(k_a), vec(r_k), vec(gn_w), vec(gn_b))


def _extract_top(work, dst_ref, count, want_rank=False):
    rank = jnp.full(work.shape, float(count), F32) if want_rank else None
    for r in range(count):
        m = jnp.max(work, axis=0, keepdims=True)
        dst_ref[r:r + 1, :] = m
        hit = work >= m
        if want_rank:
            rank = jnp.where(hit, float(r), rank)
        work = jnp.where(hit, -jnp.inf, work)
    return rank


def _peer_topk_kernel(q_ref, keys_ref, n1_ref, f1_ref, rk2_ref, e2_ref, a1_sc, a2_sc, cand_sc, top_sc):
    kk = P_TOPK
    for h in range(P_HEADS):
        sc = []
        for half in range(2):
            qh = q_ref[:, (2 * h + half) * LANES:(2 * h + half + 1) * LANES]
            sc.append(lax.dot_general(keys_ref[h, half], qh, NT_DIMS, precision=HIGHEST,
                                      preferred_element_type=F32))
        _extract_top(sc[0], a1_sc, kk)
        rank2 = _extract_top(sc[1], a2_sc, kk, want_rank=True)
        cand_sc[...] = jnp.full(cand_sc.shape, -jnp.inf, F32)
        off = 0
        for i in range(kk):
            nj = kk // (i + 1)
            cand_sc[off:off + nj, :] = a1_sc[i:i + 1, :] + a2_sc[0:nj, :]
            off += nj
        _extract_top(cand_sc[...], top_sc, kk)
        top = top_sc[...]
        best = top[0:1, :]
        tau = top[kk - 1:kk, :]
        zsum = jnp.sum(jnp.exp(top - best), axis=0, keepdims=True)
        count = jnp.zeros(sc[0].shape, F32)
        for j in range(kk):
            count = count + jnp.where(sc[0] + a2_sc[j:j + 1, :] >= tau, 1.0, 0.0)
        n1_ref[h] = count
        rk2_ref[h] = rank2.astype(BF16)
        f1_ref[h] = jnp.exp(sc[0] - a1_sc[0:1, :]) / zsum
        e2_ref[h] = jnp.exp(sc[1] - a2_sc[0:1, :]).astype(BF16)


def peer_topk(q, keys):
    t = q.shape[0]
    tt = _pick(t, (256, 128))
    n_cand = sum(P_TOPK // (i + 1) for i in range(P_TOPK))
    n_cand = -(-n_cand // 8) * 8
    big = pl.BlockSpec((P_HEADS, P_NKEYS, tt), lambda i: (0, 0, i))
    big_shape = jax.ShapeDtypeStruct((P_HEADS, P_NKEYS, t), F32)
    return pl.pallas_call(
        _peer_topk_kernel,
        grid=(t // tt,),
        in_specs=[pl.BlockSpec((tt, q.shape[1]), lambda i: (i, 0)),
                  pl.BlockSpec(keys.shape, lambda i: (0, 0, 0, 0))],
        out_specs=[big, big, big, big],
        out_shape=[big_shape, big_shape, big_shape.update(dtype=BF16), big_shape.update(dtype=BF16)],
        scratch_shapes=[pltpu.VMEM((P_TOPK, tt), F32), pltpu.VMEM((P_TOPK, tt), F32),
                        pltpu.VMEM((n_cand, tt), F32), pltpu.VMEM((P_TOPK, tt), F32)],
        compiler_params=_params("parallel"),
        name="peer_topk",
    )(q, keys)


def _peer_dense_kernel(sc_ref, x_ref, u_ref, v_ref, n1_ref, f1_ref, rk2_ref, e2_ref, o_ref, *, rows, chains):
    @pl.when(pl.program_id(1) == 0)
    def _():
        o_ref[...] = jnp.zeros(o_ref.shape, F32)

    tc = x_ref.shape[0] // chains
    hr = rows // 2
    dh = o_ref.shape[1] // 2
    tok = lambda ch: slice(ch * tc, (ch + 1) * tc)

    def activation(ch, half):
        es = slice(half * hr * P_NKEYS, (half + 1) * hr * P_NKEYS)
        return lax.dot_general(u_ref[es, :], x_ref[tok(ch), :], NT_DIMS, preferred_element_type=F32)

    def gated(ch, half, act):
        ts = tok(ch)
        sc = sc_ref[0]
        act = (act * (0.5 * sc)) * (1.0 + lax.erf(act * (sc * 2.0 ** -0.5)))
        act = act.astype(BF16)
        pk = P_NKEYS // BF16_ROWS
        parts = []
        for r in range(hr):
            row = half * hr + r
            coef = None
            for h in range(P_HEADS):
                n_row = jnp.broadcast_to(n1_ref[h, row:row + 1, ts], (BF16_ROWS, tc)).astype(BF16)
                f_row = jnp.broadcast_to(f1_ref[h, row:row + 1, ts], (BF16_ROWS, tc)).astype(BF16)
                sel = rk2_ref[h, :, :, ts] < n_row[None]
                c = jnp.where(sel, e2_ref[h, :, :, ts] * f_row[None], jnp.zeros((), BF16))
                coef = c if coef is None else coef + c
            a3 = act[r * P_NKEYS:(r + 1) * P_NKEYS, :].reshape(pk, BF16_ROWS, tc)
            parts.append((coef * a3).reshape(P_NKEYS, tc))
        return parts

    def project(ch, half, w):
        cs = slice(half * dh, (half + 1) * dh)
        o_ref[tok(ch), cs] += lax.dot_general(w, v_ref[:, cs], TN_DIMS, preferred_element_type=F32)

    acts = [activation(0, 0), activation(0, 1)]
    w_prev = None
    for ch in range(chains):
        nxt, parts = [], []
        for half in range(2):
            parts += gated(ch, half, acts[half])
            if ch + 1 < chains:
                nxt.append(activation(ch + 1, half))
            if w_prev is not None:
                project(ch - 1, half, w_prev)
        w_prev = jnp.concatenate(parts, axis=0)
        acts = nxt
    project(chains - 1, 0, w_prev)
    project(chains - 1, 1, w_prev)


def peer_dense(x, u, v, act_scale, n1, f1, rk2, e2):
    t, d = x.shape
    n = u.shape[0]
    tt = _pick(t, (1024, 512, 256, 128))
    chains = 2 if tt % 256 == 0 else 1
    rows = 8
    tn = rows * P_NKEYS
    n1r = n1.reshape(P_HEADS, P_NKEYS // rows, rows, t)
    f1r = f1.reshape(P_HEADS, P_NKEYS // rows, rows, t)
    pk = P_NKEYS // BF16_ROWS
    rk2 = rk2.reshape(P_HEADS, pk, BF16_ROWS, t)
    e2 = e2.reshape(P_HEADS, pk, BF16_ROWS, t)
    once = pl.Buffered(1)
    rowspec = pl.BlockSpec((P_HEADS, None, rows, tt), lambda i, e: (0, e, 0, i))
    fullspec = pl.BlockSpec((P_HEADS, pk, BF16_ROWS, tt), lambda i, e: (0, 0, 0, i), pipeline_mode=once)
    return pl.pallas_call(
        functools.partial(_peer_dense_kernel, rows=rows, chains=chains),
        grid=(t // tt, n // tn),
        in_specs=[pl.BlockSpec(memory_space=pltpu.SMEM),
                  pl.BlockSpec((tt, d), lambda i, e: (i, 0), pipeline_mode=once),
                  pl.BlockSpec((tn, d), lambda i, e: (e, 0)),
                  pl.BlockSpec((tn, d), lambda i, e: (e, 0)),
                  rowspec, rowspec, fullspec, fullspec],
        out_specs=pl.BlockSpec((tt, d), lambda i, e: (i, 0), pipeline_mode=once),
        out_shape=jax.ShapeDtypeStruct((t, d), F32),
        compiler_params=_params("parallel", "arbitrary"),
        name="peer_dense",
    )(act_scale.astype(F32), x, u, v, n1r, f1r, rk2, e2)


def _cast_kernel(sc_ref, x_ref, o_ref):
    o_ref[...] = (x_ref[...].astype(F32) * sc_ref[0]).astype(o_ref.dtype)


def _stacked_spec(a, layer, tr):
    if layer is None:
        return pl.BlockSpec((tr, a.shape[-1]), lambda i: (i, 0))
    return pl.BlockSpec((None, tr, a.shape[-1]), lambda i: (layer, i, 0))


def scaled_cast(a, dtype, scale=None, layer=None):
    r, c = a.shape[-2:]
    tr = _pick(r, (512, 256, 128, 64, 32))
    scale = jnp.ones((1,), F32) if scale is None else scale.reshape(1).astype(F32)
    return pl.pallas_call(
        _cast_kernel,
        grid=(r // tr,),
        in_specs=[pl.BlockSpec(memory_space=pltpu.SMEM), _stacked_spec(a, layer, tr)],
        out_specs=pl.BlockSpec((tr, c), lambda i: (i, 0)),
        out_shape=jax.ShapeDtypeStruct((r, c), dtype),
        compiler_params=_params("parallel"),
        name="scaled_cast",
    )(scale, a)


def _absmax_kernel(x_ref, o_ref):
    m = jnp.max(jnp.abs(x_ref[...].astype(F32)), axis=0, keepdims=True)
    part = m[:, 0:LANES]
    for g in range(1, m.shape[1] // LANES):
        part = jnp.maximum(part, m[:, g * LANES:(g + 1) * LANES])
    o_ref[...] = jnp.broadcast_to(part, o_ref.shape)


def absmax(a, layer=None):
    r, c = a.shape[-2:]
    tr = _pick(r, (512, 256, 128, 64, 32))
    part = pl.pallas_call(
        _absmax_kernel,
        grid=(r // tr,),
        in_specs=[_stacked_spec(a, layer, tr)],
        out_specs=pl.BlockSpec((8, LANES), lambda i: (i, 0)),
        out_shape=jax.ShapeDtypeStruct((r // tr * 8, LANES), F32),
        compiler_params=_params("parallel"),
        name="absmax",
    )(a)
    return jnp.max(part)


def _fp8_scaled(a, layer=None):
    k = jnp.floor(jnp.log2(FP8_TARGET_MAX / jnp.maximum(absmax(a, layer), 1e-30)))
    scale = jnp.exp2(k)
    return scaled_cast(a, FP8, scale, layer), 1.0 / scale


def norm_fp8_scale(g, d):
    bound = math.sqrt(d) * jnp.max(jnp.abs(g)).astype(F32)
    return jnp.exp2(jnp.floor(jnp.log2(FP8_TARGET_MAX / jnp.maximum(bound, 1e-30))))


def peer_ffn(hn, x8, inv_x, w_q, keys, u, v, layer):
    q = matmul([hn], [scaled_cast(w_q, BF16, layer=layer)], F32)
    n1, f1, rk2, e2 = peer_topk(q, keys)
    u8, inv_u = _fp8_scaled(u, layer)
    return peer_dense(x8, u8, scaled_cast(v, BF16, layer=layer), (inv_x * inv_u).reshape(1), n1, f1, rk2, e2)


def even_mixer(hn, w_in, w_out, lam_params, subln_gain, lam_init, w_gate, b_gate, gla_norm, batch, seq, residual=None):
    na = w_out.shape[0] // 2
    a_heads = na // A_V_DIM
    b_heads = na // B_V_DIM
    sizes = (2 * a_heads * A_QK_DIM, 2 * a_heads * A_QK_DIM, na, b_heads * B_K_DIM, b_heads * B_K_DIM, na,
             w_gate.shape[0], na)
    names = ("aq", "ak", "av", "bq", "bk", "bv", "lo", "og")
    start, off = {}, 0
    for nm, sz in zip(names, sizes):
        start[nm] = off
        off += sz
    w_main = jnp.concatenate([w_in[:, :start["lo"]], w_in[:, start["og"]:]], axis=1).astype(BF16)
    cols = dict(start)
    cols["og"] = start["lo"]
    w_lo = _pad_cols(w_in[:, start["lo"]:start["og"]], LANES).astype(BF16)
    p = matmul([hn], [w_main], BF16)
    p_lo = matmul([hn], [w_lo], F32)
    qk = rope(p, rope_tables(seq, A_ROT), seq, width=start["av"], col_block=0, half=A_ROT // 2)
    lp = lam_params.astype(F32)
    lam = jnp.exp(jnp.sum(lp[0] * lp[1])) - jnp.exp(jnp.sum(lp[2] * lp[3])) + lam_init
    o_a = diff_attention(qk, p, lam, subln_gain, lam_init, batch, seq, start["av"])
    w_gate_p = jnp.pad(w_gate.astype(F32), ((0, LANES - w_gate.shape[0]), (0, 0)))
    o_b = gla(p, p_lo, w_gate_p, b_gate.astype(F32), gla_norm.astype(F32), batch, seq, cols)
    return matmul([o_a, o_b], scaled_cast(w_out, BF16), F32, residual, stacked=True)


def odd_mixer(hn, w_in, w_out, q_norm, kv_norm, w_uq, w_ukv, mu, w0, w2, a0, a2, g2,
              k_k, k_a, r_k, gn_w, gn_b, batch, seq, residual=None):
    q_rank, kv_rank = q_norm.shape[0], kv_norm.shape[0]
    width = w0.shape[0]
    rw, ra, rg = w2.shape[0], a2.shape[0], g2.shape[0]
    c_cols = q_rank + kv_rank + C_ROPE
    heads = w_uq.shape[1] // (C_NOPE + C_ROPE)
    c_pad = -(-(q_rank + kv_rank + LANES) // 256) * 256
    w_c = _pad_cols(w_in[:, :c_cols], c_pad).astype(BF16)
    pad_l = lambda m: -(-m // LANES) * LANES
    d0 = c_cols
    segs, mus, off = [], [], d0
    for sz in (width, width, width, rw, ra, rg):
        segs.append(_pad_cols(w_in[:, off:off + sz], pad_l(sz)))
        mus.append(jnp.pad(mu[off - d0:off - d0 + sz], (0, pad_l(sz) - sz)))
        off += sz
    w_d = jnp.concatenate(segs, axis=1).astype(BF16)
    mu_d = jnp.concatenate(mus).astype(F32)
    pad_r = lambda w: jnp.pad(w.astype(F32), ((0, pad_l(w.shape[0]) - w.shape[0]), (0, 0)))

    p_c = matmul([hn], [w_c], F32)
    p_d = matmul([hn], [w_d], F32)

    cq = add_rmsnorm([p_c], q_norm, BF16, width=q_rank, col_block=0)
    ckv = add_rmsnorm([p_c], kv_norm, BF16, width=kv_rank, col_block=q_rank // kv_rank)
    wq3 = w_uq.reshape(q_rank, heads, C_NOPE + C_ROPE)
    wq_nope = wq3[:, :, :C_NOPE].reshape(q_rank, heads * C_NOPE)
    wq_rope = jnp.pad(wq3[:, :, C_NOPE:], ((0, 0), (0, 0), (0, LANES - C_ROPE))).reshape(q_rank, heads * LANES)
    q_all = matmul([cq], [jnp.concatenate([wq_nope, wq_rope], axis=1).astype(BF16)], BF16)
    wkv3 = w_ukv.reshape(kv_rank, heads, C_NOPE + C_V)
    w_kv = jnp.concatenate([wkv3[:, :, :C_NOPE].reshape(kv_rank, heads * C_NOPE),
                            wkv3[:, :, C_NOPE:].reshape(kv_rank, heads * C_V)], axis=1).astype(BF16)
    kv = matmul([ckv], [w_kv], BF16)
    tables = rope_tables(seq, C_ROPE)
    q_rope = rope(q_all, tables, seq, width=heads * LANES, col_block=1, half=C_ROPE // 2)
    k_rope = rope(p_c, tables, seq, width=LANES, col_block=(q_rank + kv_rank) // LANES, half=C_ROPE // 2)
    o_c = mla_attention(q_all, q_rope, kv, k_rope, batch, seq)

    r, k, v, wl, a, g = rwkv_prep(p_d, mu_d, w0.astype(F32), pad_r(w2), a0.astype(F32), pad_r(a2), pad_r(g2),
                                  batch, seq)
    o_d = rwkv_scan(r, k, v, wl, a, g, k_k, k_a, r_k, gn_w, gn_b, batch, seq)
    return matmul([o_c, o_d], scaled_cast(w_out, BF16), F32, residual, stacked=True)


def kernel(x, norm_mix, norm_ffn, norm_final, even_w_in, even_w_out, diff_lambda, diff_subln, gla_w_gate, gla_b_gate, gla_norm, odd_w_in, odd_w_out, mla_q_norm, mla_kv_norm, mla_w_uq, mla_w_ukv, rwkv_mu, rwkv_w0, rwkv_w2, rwkv_a0, rwkv_a2, rwkv_g2, rwkv_k_k, rwkv_k_a, rwkv_r_k, rwkv_gn_w, rwkv_gn_b, peer_w_q, peer_keys, peer_u, peer_v):
    batch, seq, d = x.shape
    depth = norm_mix.shape[0]
    h = x.reshape(batch * seq, d)
    pending = []
    for layer in range(depth):
        j = layer // 2
        if pending:
            h, hn = add_rmsnorm([h] + pending, norm_mix[layer], BF16, emit_sum=True)
        else:
            hn = add_rmsnorm([h], norm_mix[layer], BF16)
        if layer % 2 == 0:
            lam_init = 0.8 - 0.6 * math.exp(-0.3 * layer)
            h = even_mixer(hn, even_w_in[j], even_w_out[j], diff_lambda[j], diff_subln[j], lam_init,
                           gla_w_gate[j], gla_b_gate[j], gla_norm[j], batch, seq, residual=h)
        else:
            h = odd_mixer(hn, odd_w_in[j], odd_w_out[j], mla_q_norm[j], mla_kv_norm[j], mla_w_uq[j],
                          mla_w_ukv[j], rwkv_mu[j], rwkv_w0[j], rwkv_w2[j], rwkv_a0[j], rwkv_a2[j],
                          rwkv_g2[j], rwkv_k_k[j], rwkv_k_a[j], rwkv_r_k[j], rwkv_gn_w[j], rwkv_gn_b[j],
                          batch, seq, residual=h)
        x_scale = norm_fp8_scale(norm_ffn[layer], d)
        hn2, x8 = add_rmsnorm([h], norm_ffn[layer], BF16, fp8_scale=x_scale)
        pending = [peer_ffn(hn2, x8, 1.0 / x_scale, peer_w_q, peer_keys[layer], peer_u, peer_v, layer)]
    out = add_rmsnorm([h] + pending, norm_final, F32)
    return out.reshape(batch, seq, d).astype(x.dtype)
```

```python
import functools
import math

import jax
import jax.numpy as jnp
from jax import lax
from jax.experimental import pallas as pl
from jax.experimental.pallas import tpu as pltpu

F32 = jnp.float32
BF16 = jnp.bfloat16
FP8 = jnp.float8_e4m3fn
FP8_TARGET_MAX = 256.0
HIGHEST = lax.Precision.HIGHEST

LANES = 128
BF16_ROWS = 16
VMEM_LIMIT = 56 * 1024 * 1024

EPS = 1e-6
ROPE_THETA = 500000.0
A_QK_DIM = 128
A_V_DIM = 256
A_ROT = A_QK_DIM // 4
B_V_DIM = 512
B_K_DIM = 256
B_GATE_TAU = 16.0
GLA_CHUNK = 64
C_NOPE = 128
C_ROPE = 64
C_V = 128
D_HEAD = 64
D_GN_EPS = 64e-5
RWKV_CHUNK = 64
ATTN_BLOCKS = (1024, 512, 256, 128)
ATTN_SUB = 4
RWKV_HEADS_PER_STEP = 4
RWKV_GROUPS_PER_STEP = 4
P_HEADS = 8
P_NKEYS = 128
P_TOPK = 16

NT_DIMS = (((1,), (1,)), ((), ()))
TN_DIMS = (((0,), (0,)), ((), ()))


def _params(*sem):
    return pltpu.CompilerParams(dimension_semantics=sem, vmem_limit_bytes=VMEM_LIMIT)


def _pick(n, cands):
    for c in cands:
        if n % c == 0:
            return c
    raise ValueError(f"no tile in {cands} divides {n}")


def _pad_cols(w, n):
    return jnp.pad(w, ((0, 0), (0, n - w.shape[1])))


def _norm_kernel(*refs, eps, n_add, emit_sum, emit_fp8):
    xs = refs[:n_add]
    g_ref = refs[n_add]
    outs = list(refs[n_add + 1 + emit_fp8:])
    x = xs[0][...].astype(F32)
    for r in xs[1:]:
        x = x + r[...].astype(F32)
    y = x * lax.rsqrt(jnp.mean(x * x, axis=-1, keepdims=True) + eps) * g_ref[...]
    if emit_sum:
        outs.pop(0)[...] = x
    o_ref = outs.pop(0)
    o_ref[...] = y.astype(o_ref.dtype)
    if emit_fp8:
        outs.pop(0)[...] = (y * refs[n_add + 1][0]).astype(FP8)


def add_rmsnorm(xs, g, out_dtype, *, emit_sum=False, fp8_scale=None, width=None, col_block=0, eps=EPS):
    t = xs[0].shape[0]
    width = width or xs[0].shape[1]
    emit_fp8 = fp8_scale is not None
    tr = _pick(t, (128, 64, 32, 16, 8))
    spec = pl.BlockSpec((tr, width), lambda i: (i, col_block))
    ospec = pl.BlockSpec((tr, width), lambda i: (i, 0))
    out_shape = [jax.ShapeDtypeStruct((t, width), out_dtype)]
    out_specs = [ospec]
    if emit_sum:
        out_shape = [jax.ShapeDtypeStruct((t, width), F32)] + out_shape
        out_specs = [ospec, ospec]
    extra_in, extra_specs = [], []
    if emit_fp8:
        out_shape = out_shape + [jax.ShapeDtypeStruct((t, width), FP8)]
        out_specs = out_specs + [ospec]
        extra_in = [fp8_scale.reshape(1).astype(F32)]
        extra_specs = [pl.BlockSpec(memory_space=pltpu.SMEM)]
    res = pl.pallas_call(
        functools.partial(_norm_kernel, eps=eps, n_add=len(xs), emit_sum=emit_sum, emit_fp8=emit_fp8),
        grid=(t // tr,),
        in_specs=[spec] * len(xs) + [pl.BlockSpec((1, width), lambda i: (0, 0))] + extra_specs,
        out_specs=out_specs,
        out_shape=out_shape,
        compiler_params=_params("parallel"),
        name="add_rmsnorm",
    )(*xs, g.reshape(1, width).astype(F32), *extra_in)
    return res if len(res) > 1 else res[0]


def _mm_kernel(*refs, n_in, has_residual):
    o_ref = refs[-1]
    acc = jnp.dot(refs[0][...], refs[n_in][...], preferred_element_type=F32)
    for i in range(1, n_in):
        acc = acc + jnp.dot(refs[i][...], refs[n_in + i][...], preferred_element_type=F32)
    if has_residual:
        acc = acc + refs[2 * n_in][...]
    o_ref[...] = acc.astype(o_ref.dtype)


def matmul(xs, ws, out_dtype, residual=None, stacked=False):
    m = xs[0].shape[0]
    if stacked:
        kw = xs[0].shape[1]
        ws = [ws] * len(xs)
    n = ws[0].shape[1]
    tm = _pick(m, (1024, 512, 256, 128, 64, 32, 16))
    tn = _pick(n, (512, 768, 640, 256, 128))
    in_specs = [pl.BlockSpec((tm, x.shape[1]), lambda i, j: (i, 0)) for x in xs]
    if stacked:
        in_specs += [pl.BlockSpec((kw, tn), functools.partial(lambda i, j, b: (b, j), b=b)) for b in range(len(xs))]
    else:
        in_specs += [pl.BlockSpec((w.shape[0], tn), lambda i, j: (0, j)) for w in ws]
    extra = []
    if residual is not None:
        in_specs.append(pl.BlockSpec((tm, tn), lambda i, j: (i, j)))
        extra = [residual]
    return pl.pallas_call(
        functools.partial(_mm_kernel, n_in=len(xs), has_residual=residual is not None),
        grid=(m // tm, n // tn),
        in_specs=in_specs,
        out_specs=pl.BlockSpec((tm, tn), lambda i, j: (i, j)),
        out_shape=jax.ShapeDtypeStruct((m, n), out_dtype),
        compiler_params=_params("parallel", "arbitrary"),
        name="matmul",
    )(*xs, *ws, *extra)


def rope_tables(seq, rot_dim):
    half = rot_dim // 2
    inv = 1.0 / (ROPE_THETA ** (jnp.arange(0, rot_dim, 2, dtype=F32) / rot_dim))
    ang = jnp.arange(seq, dtype=F32)[:, None] * inv[None, :]
    cos, sin = jnp.cos(ang), jnp.sin(ang)
    z = jnp.zeros((seq, LANES - 2 * half), F32)
    zh = jnp.zeros((seq, half), F32)
    c = jnp.concatenate([cos, cos, jnp.ones_like(z)], axis=1)
    s_up = jnp.concatenate([zh, sin, z], axis=1)
    s_dn = jnp.concatenate([-sin, zh, z], axis=1)
    return c, s_up, s_dn


def _rope_kernel(x_ref, c_ref, su_ref, sd_ref, o_ref, *, half, groups):
    c, su, sd = c_ref[...], su_ref[...], sd_ref[...]
    for g in range(groups):
        sl = slice(g * LANES, (g + 1) * LANES)
        x = x_ref[:, sl].astype(F32)
        y = x * c + pltpu.roll(x, half, 1) * su + pltpu.roll(x, LANES - half, 1) * sd
        o_ref[:, sl] = y.astype(o_ref.dtype)


def rope(x, tables, seq, *, width, col_block, half):
    t = x.shape[0]
    tr = _pick(seq, (256, 128, 64, 32, 16))
    nb = seq // tr
    tspec = pl.BlockSpec((tr, LANES), lambda i: (i % nb, 0))
    return pl.pallas_call(
        functools.partial(_rope_kernel, half=half, groups=width // LANES),
        grid=(t // tr,),
        in_specs=[pl.BlockSpec((tr, width), lambda i: (i, col_block)), tspec, tspec, tspec],
        out_specs=pl.BlockSpec((tr, width), lambda i: (i, 0)),
        out_shape=jax.ShapeDtypeStruct((t, width), BF16),
        compiler_params=_params("parallel"),
        name="rope",
    )(x, *tables)


LOG2E = 1.4426950408889634


def _scaled_q(q, scale):
    return (q.astype(F32) * (scale * LOG2E)).astype(BF16)


def _softmax_chains(chains):
    s = [lax.dot_general(q, k, NT_DIMS, preferred_element_type=F32) for q, k, *_ in chains]
    def masked(x, tail):
        if tail is None:
            return x
        w = tail.shape[1]
        last = jnp.where(tail, x[:, x.shape[1] - w:], -jnp.inf)
        return last if w == x.shape[1] else jnp.concatenate([x[:, :x.shape[1] - w], last], axis=1)

    s = [masked(x, c[6]) for x, c in zip(s, chains)]
    m_prev = [c[3][...] for c in chains]
    m_new = [jnp.maximum(mp, jnp.max(x, axis=-1, keepdims=True)) for mp, x in zip(m_prev, s)]
    alpha = [jnp.exp2(mp - mn) for mp, mn in zip(m_prev, m_new)]
    p = [jnp.exp2(x - mn) for x, mn in zip(s, m_new)]
    pv = [jnp.dot(x.astype(c[2].dtype), c[2], preferred_element_type=F32) for x, c in zip(p, chains)]
    for c, a, x, y, mn in zip(chains, alpha, p, pv, m_new):
        c[4][...] = a * c[4][...] + jnp.sum(x, axis=-1, keepdims=True)
        c[5][...] = a * c[5][...] + y
        c[3][...] = mn


def _causal_sweep(i, tq, step):
    def body(j, carry):
        step(pl.ds(pl.multiple_of(j * tq, tq), tq), False)
        return carry
    lax.fori_loop(0, i, body, 0)
    step(pl.ds(pl.multiple_of(i * tq, tq), tq), True)


def _row_groups(q, k, v, m_sc, l_sc, acc_sc, diagonal):
    rs = q.shape[0] // ATTN_SUB
    chains = []
    tri = _tril(rs)
    for g in range(ATTN_SUB):
        qg = q[g * rs:(g + 1) * rs]
        if diagonal:
            r = g * rs + lax.broadcasted_iota(jnp.int32, (rs, k.shape[0]), 0)
            c = lax.broadcasted_iota(jnp.int32, (rs, k.shape[0]), 1)
            chains.append((qg, k, v, m_sc.at[g], l_sc.at[g], acc_sc.at[g], c <= r))
        else:
            chains.append((qg, k, v, m_sc.at[g], l_sc.at[g], acc_sc.at[g], None))
    return chains


def _diff_attn_kernel(lam_ref, q_ref, k_ref, v_ref, g_ref, o_ref, m_sc, l_sc, acc_sc, *, scale, tq, out_scale, eps):
    i = pl.program_id(2)
    m_sc[...] = jnp.full(m_sc.shape, -jnp.inf, F32)
    l_sc[...] = jnp.zeros(l_sc.shape, F32)
    acc_sc[...] = jnp.zeros(acc_sc.shape, F32)
    qs = [_scaled_q(q_ref[:, mp * A_QK_DIM:(mp + 1) * A_QK_DIM], scale) for mp in range(2)]

    def step(rows, diagonal):
        v = v_ref[rows, :]
        chains = []
        for mp in range(2):
            k = k_ref[rows, mp * A_QK_DIM:(mp + 1) * A_QK_DIM]
            chains += _row_groups(qs[mp], k, v, m_sc.at[mp], l_sc.at[mp], acc_sc.at[mp], diagonal)
        _softmax_chains(chains)

    _causal_sweep(i, tq, step)
    rs = tq // ATTN_SUB
    for g in range(ATTN_SUB):
        o = acc_sc[0, g] / l_sc[0, g] - lam_ref[0] * (acc_sc[1, g] / l_sc[1, g])
        o = o * lax.rsqrt(jnp.mean(o * o, axis=-1, keepdims=True) + eps) * g_ref[...] * out_scale
        o_ref[g * rs:(g + 1) * rs, :] = o.astype(o_ref.dtype)


def diff_attention(qk, p, lam, subln_gain, lam_init, batch, seq, v_col):
    t = qk.shape[0]
    heads = qk.shape[1] // (4 * A_QK_DIM)
    hw = 2 * A_QK_DIM
    tq = _pick(seq, ATTN_BLOCKS)
    rs = tq // ATTN_SUB
    nq = seq // tq
    vb = v_col // A_V_DIM
    kern = functools.partial(_diff_attn_kernel, scale=A_QK_DIM ** -0.5, tq=tq, out_scale=1.0 - lam_init, eps=EPS)
    return pl.pallas_call(
        kern,
        grid=(batch, heads, nq),
        in_specs=[
            pl.BlockSpec(memory_space=pltpu.SMEM),
            pl.BlockSpec((tq, hw), lambda b, h, i: (b * nq + i, h)),
            pl.BlockSpec((seq, hw), lambda b, h, i: (b, heads + h)),
            pl.BlockSpec((seq, A_V_DIM), lambda b, h, i: (b, vb + h)),
            pl.BlockSpec((1, A_V_DIM), lambda b, h, i: (0, 0)),
        ],
        out_specs=pl.BlockSpec((tq, A_V_DIM), lambda b, h, i: (b * nq + i, h)),
        out_shape=jax.ShapeDtypeStruct((t, heads * A_V_DIM), BF16),
        scratch_shapes=[pltpu.VMEM((2, ATTN_SUB, rs, 1), F32), pltpu.VMEM((2, ATTN_SUB, rs, 1), F32),
                        pltpu.VMEM((2, ATTN_SUB, rs, A_V_DIM), F32)],
        compiler_params=_params("parallel", "parallel", "arbitrary"),
        name="diff_attention",
    )(lam.reshape(1).astype(F32), qk, qk, p, subln_gain.reshape(1, A_V_DIM).astype(F32))


def _mla_attn_kernel(qn_ref, qr_ref, kn_ref, kr_ref, v_ref, o_ref, m_sc, l_sc, acc_sc, *, scale, tq):
    i = pl.program_id(2)
    m_sc[...] = jnp.full(m_sc.shape, -jnp.inf, F32)
    l_sc[...] = jnp.zeros(l_sc.shape, F32)
    acc_sc[...] = jnp.zeros(acc_sc.shape, F32)
    q = _scaled_q(jnp.concatenate([qn_ref[...], qr_ref[...]], axis=-1), scale)

    def step(rows, diagonal):
        k = jnp.concatenate([kn_ref[rows, :], kr_ref[rows, :]], axis=-1)
        _softmax_chains(_row_groups(q, k, v_ref[rows, :], m_sc, l_sc, acc_sc, diagonal))

    _causal_sweep(i, tq, step)
    rs = tq // ATTN_SUB
    for g in range(ATTN_SUB):
        o_ref[g * rs:(g + 1) * rs, :] = (acc_sc[g] / l_sc[g]).astype(o_ref.dtype)


def mla_attention(q_all, q_rope, kv, k_rope, batch, seq):
    t = kv.shape[0]
    heads = kv.shape[1] // (C_NOPE + C_V)
    tq = _pick(seq, ATTN_BLOCKS)
    rs = tq // ATTN_SUB
    nq = seq // tq
    qmap = lambda b, h, i: (b * nq + i, h)
    return pl.pallas_call(
        functools.partial(_mla_attn_kernel, scale=(C_NOPE + C_ROPE) ** -0.5, tq=tq),
        grid=(batch, heads, nq),
        in_specs=[
            pl.BlockSpec((tq, C_NOPE), qmap),
            pl.BlockSpec((tq, LANES), qmap),
            pl.BlockSpec((seq, C_NOPE), lambda b, h, i: (b, h)),
            pl.BlockSpec((seq, LANES), lambda b, h, i: (b, 0)),
            pl.BlockSpec((seq, C_V), lambda b, h, i: (b, heads + h)),
        ],
        out_specs=pl.BlockSpec((tq, C_V), qmap),
        out_shape=jax.ShapeDtypeStruct((t, heads * C_V), BF16),
        scratch_shapes=[pltpu.VMEM((ATTN_SUB, rs, 1), F32), pltpu.VMEM((ATTN_SUB, rs, 1), F32),
                        pltpu.VMEM((ATTN_SUB, rs, C_V), F32)],
        compiler_params=_params("parallel", "parallel", "arbitrary"),
        name="mla_attention",
    )(q_all, q_rope, kv, k_rope, kv)


def _tril(n, strict=False):
    r = lax.broadcasted_iota(jnp.int32, (n, n), 0)
    c = lax.broadcasted_iota(jnp.int32, (n, n), 1)
    return (r > c) if strict else (r >= c)


def _gla_kernel(q_ref, k_ref, v_ref, og_ref, lo_ref, wg_ref, bg_ref, gn_ref, o_ref, st_sc, *, scale, eps, chunk, heads):
    @pl.when(pl.program_id(1) == 0)
    def _():
        st_sc[...] = jnp.zeros(st_sc.shape, F32)

    n = chunk
    n_chunks = q_ref.shape[0] // n
    dk, dv = B_K_DIM, B_V_DIM
    each = lambda f, *cols: [f(*a) for a in zip(*cols)]
    tri = _tril(n)
    trif = tri.astype(F32)
    rows = [slice(c * n, (c + 1) * n) for c in range(n_chunks)]
    z = [jnp.dot(lo_ref[r, :], wg_ref[...], precision=HIGHEST, preferred_element_type=F32) + bg_ref[...] for r in rows]
    log_a = each(lambda x: (jnp.minimum(x, 0.0) - jnp.log1p(jnp.exp(-jnp.abs(x)))) / B_GATE_TAU, z)
    b_all = each(lambda x: jnp.dot(trif, x, precision=HIGHEST, preferred_element_type=F32), log_a)
    ids = [(c, h) for c in range(n_chunks) for h in range(heads)]
    kl = [slice(h * dk, (h + 1) * dk) for _, h in ids]
    vl = [slice(h * dv, (h + 1) * dv) for _, h in ids]
    rw = [rows[c] for c, _ in ids]
    b = [b_all[c][:, s] for (c, _), s in zip(ids, kl)]
    b_last = each(lambda x: x[n - 1:n, :], b)
    q = [q_ref[r, s].astype(F32) * scale for r, s in zip(rw, kl)]
    k = [k_ref[r, s].astype(F32) for r, s in zip(rw, kl)]
    v = [v_ref[r, s] for r, s in zip(rw, vl)]
    qe = each(lambda x, bb: (x * jnp.exp(bb)).astype(BF16), q, b)
    ke = each(lambda x, bb: (x * jnp.exp(-bb)).astype(BF16), k, b)
    kend = each(lambda x, bb, bl: (x * jnp.exp(bl - bb)).astype(BF16), k, b, b_last)
    attn = each(lambda a, c: jnp.where(tri, lax.dot_general(a, c, NT_DIMS, preferred_element_type=F32), 0.0).astype(BF16),
                qe, ke)
    o_intra = each(lambda a, vv: jnp.dot(a, vv, preferred_element_type=F32), attn, v)
    decay = each(jnp.exp, b_last)
    outs = [None] * len(ids)
    for c in range(n_chunks):
        idx = [i for i, (cc, _) in enumerate(ids) if cc == c]
        st = [st_sc[ids[i][1]] for i in idx]
        o_inter = [lax.dot_general(qe[i], s.astype(BF16), NT_DIMS, preferred_element_type=F32) for i, s in zip(idx, st)]
        new = [s * decay[i] + lax.dot_general(v[i], kend[i], TN_DIMS, preferred_element_type=F32)
               for i, s in zip(idx, st)]
        for i, s_new, oi in zip(idx, new, o_inter):
            st_sc[ids[i][1]] = s_new
            outs[i] = oi + o_intra[i]
    for r, s, o in zip(rw, vl, outs):
        o = o * lax.rsqrt(jnp.mean(o * o, axis=-1, keepdims=True) + eps) * gn_ref[...]
        og = og_ref[r, s].astype(F32)
        o_ref[r, s] = (o * (og * jax.nn.sigmoid(og))).astype(o_ref.dtype)


def gla(p, p_lo, w_gate, b_gate, gn, batch, seq, cols):
    t = p.shape[0]
    heads = w_gate.shape[1] // B_K_DIM
    ch = GLA_CHUNK
    tb = _pick(seq, (2 * ch, ch))
    nb = seq // tb
    kw, vw = heads * B_K_DIM, heads * B_V_DIM
    row = lambda b, c: b * nb + c
    return pl.pallas_call(
        functools.partial(_gla_kernel, scale=B_K_DIM ** -0.5, eps=EPS, chunk=ch, heads=heads),
        grid=(batch, nb),
        in_specs=[
            pl.BlockSpec((tb, kw), lambda b, c: (row(b, c), cols["bq"] // kw)),
            pl.BlockSpec((tb, kw), lambda b, c: (row(b, c), cols["bk"] // kw)),
            pl.BlockSpec((tb, vw), lambda b, c: (row(b, c), cols["bv"] // vw)),
            pl.BlockSpec((tb, vw), lambda b, c: (row(b, c), cols["og"] // vw)),
            pl.BlockSpec((tb, LANES), lambda b, c: (row(b, c), 0)),
            pl.BlockSpec((LANES, kw), lambda b, c: (0, 0)),
            pl.BlockSpec((1, kw), lambda b, c: (0, 0)),
            pl.BlockSpec((1, B_V_DIM), lambda b, c: (0, 0)),
        ],
        out_specs=pl.BlockSpec((tb, vw), lambda b, c: (row(b, c), 0)),
        out_shape=jax.ShapeDtypeStruct((t, vw), BF16),
        scratch_shapes=[pltpu.VMEM((heads, B_V_DIM, B_K_DIM), F32)],
        compiler_params=_params("parallel", "arbitrary"),
        name="gla",
    )(p, p, p, p, p_lo, w_gate, b_gate.reshape(1, -1), gn.reshape(1, -1))


def _rwkv_prep_kernel(pd_ref, mu_ref, w0_ref, w2_ref, a0_ref, a2_ref, g2_ref,
                      r_ref, k_ref, v_ref, wl_ref, a_ref, g_ref, carry_sc, *, width, rw, ra):
    @pl.when(pl.program_id(1) == 0)
    def _():
        carry_sc[...] = jnp.zeros(carry_sc.shape, F32)

    x = pd_ref[...]
    n = x.shape[0]
    first = lax.broadcasted_iota(jnp.int32, x.shape, 0) == 0
    prev = jnp.where(first, carry_sc[...], pltpu.roll(x, 1, 0))
    carry_sc[...] = x[n - 1:n, :]
    xs = x + (prev - x) * mu_ref[...]
    r_ref[...] = xs[:, 0:width]
    k_ref[...] = xs[:, width:2 * width]
    v_ref[...] = xs[:, 2 * width:3 * width]
    c0 = 3 * width
    xw, xa, xg = xs[:, c0:c0 + rw], xs[:, c0 + rw:c0 + rw + ra], xs[:, c0 + rw + ra:]
    zw = w0_ref[...] + jnp.dot(jnp.tanh(xw), w2_ref[...], precision=HIGHEST, preferred_element_type=F32)
    w = -(jnp.maximum(-zw, 0.0) + jnp.log1p(jnp.exp(-jnp.abs(zw)))) - 0.5
    wl_ref[...] = -jnp.exp(w)
    za = a0_ref[...] + jnp.dot(xa, a2_ref[...], precision=HIGHEST, preferred_element_type=F32)
    a_ref[...] = jax.nn.sigmoid(za)
    g_ref[...] = jnp.dot(jax.nn.sigmoid(xg), g2_ref[...], precision=HIGHEST, preferred_element_type=F32)


def rwkv_prep(pd, mu, w0, w2, a0, a2, g2, batch, seq):
    t, cols = pd.shape
    width = w0.shape[-1]
    rw, ra, rg = w2.shape[0], a2.shape[0], g2.shape[0]
    tb = _pick(seq, (256, 128, 64, 32, 16, 8))
    nb = seq // tb
    full = lambda shape: pl.BlockSpec(shape, lambda b, s: (0, 0))
    ospec = pl.BlockSpec((tb, width), lambda b, s: (b * nb + s, 0))
    return pl.pallas_call(
        functools.partial(_rwkv_prep_kernel, width=width, rw=rw, ra=ra),
        grid=(batch, nb),
        in_specs=[pl.BlockSpec((tb, cols), lambda b, s: (b * nb + s, 0)), full((1, cols)), full((1, width)),
                  full((rw, width)), full((1, width)), full((ra, width)), full((rg, width))],
        out_specs=[ospec] * 6,
        out_shape=[jax.ShapeDtypeStruct((t, width), F32)] * 6,
        scratch_shapes=[pltpu.VMEM((1, cols), F32)],
        compiler_params=_params("parallel", "arbitrary"),
        name="rwkv_prep",
    )(pd, mu.reshape(1, cols), w0.reshape(1, width), w2, a0.reshape(1, width), a2, g2)


def _rwkv_scan_kernel(r_ref, k_ref, v_ref, wl_ref, a_ref, g_ref, kk_ref, ka_ref, rk_ref, gw_ref, gb_ref,
                      o_ref, st_sc, *, heads, chunk, gn_eps):
    @pl.when(pl.program_id(2) == 0)
    def _():
        st_sc[...] = jnp.zeros(st_sc.shape, F32)

    n = chunk
    hw = heads * D_HEAD
    rows = heads * n
    ri = lax.broadcasted_iota(jnp.int32, (rows, hw), 0)
    ci = lax.broadcasted_iota(jnp.int32, (rows, hw), 1)
    same_head = (ri // n) == (ci // D_HEAD)
    rt_i = lax.broadcasted_iota(jnp.int32, (rows, rows), 0)
    ct_i = lax.broadcasted_iota(jnp.int32, (rows, rows), 1)
    strict = (rt_i % n) > (ct_i % n)
    incl = (rt_i % n) >= (ct_i % n)
    eye = (rt_i == ct_i).astype(F32)
    li = lax.broadcasted_iota(jnp.int32, (hw, hw), 0)
    lj = lax.broadcasted_iota(jnp.int32, (hw, hw), 1)
    head_ones = ((li // D_HEAD) == (lj // D_HEAD)).astype(BF16)
    tri_n = _tril(n).astype(F32)

    def to_blk(x):
        return jnp.where(same_head, jnp.concatenate([x] * heads, axis=0), 0.0).astype(BF16)

    def from_blk(y):
        out = y[0:n]
        for h in range(1, heads):
            out = out + y[h * n:(h + 1) * n]
        return out

    def head_sum(x):
        hi = x.astype(BF16)
        lo = (x - hi.astype(F32)).astype(BF16)
        s = jnp.dot(jnp.concatenate([hi, lo], axis=0), head_ones, preferred_element_type=F32)
        return s[:n] + s[n:]

    bdot = functools.partial(jnp.dot, preferred_element_type=F32)
    each = lambda f, *cols: [f(*a) for a in zip(*cols)]
    n_chunks, n_groups = r_ref.shape[0] // n, r_ref.shape[1] // hw
    ids = [(c, gi) for c in range(n_chunks) for gi in range(n_groups)]
    sls = [slice(c * n, (c + 1) * n) for c, _ in ids]
    lns = [slice(gi * hw, (gi + 1) * hw) for _, gi in ids]
    load = lambda ref: [ref[sl, ln] for sl, ln in zip(sls, lns)]
    par = lambda ref: [ref[:, ln] for ln in lns]
    r, k, v, wl, a_s = load(r_ref), load(k_ref), load(v_ref), load(wl_ref), load(a_ref)
    cum = each(lambda x: jnp.dot(tri_n, x, precision=HIGHEST, preferred_element_type=F32), wl)
    kk = each(lambda x, w: x * w, k, par(kk_ref))
    ksq = each(lambda x: head_sum(x * x), kk)
    kk = each(lambda x, s: x * lax.rsqrt(jnp.maximum(s, 1e-24)), kk, ksq)
    ke = each(lambda x, a, w: x * (1.0 + (a - 1.0) * w), k, a_s, par(ka_ref))
    cum_last = each(lambda x: x[n - 1:n, :], cum)
    e_neg = each(lambda x: jnp.exp(-x), cum)
    e_end = each(lambda x, xl: jnp.exp(xl - x), cum, cum_last)
    kka = each(lambda x, a: x * a, kk, a_s)
    lhs = each(lambda x, cu, w, rr: jnp.concatenate([to_blk(-x * jnp.exp(cu - w)), to_blk(rr * jnp.exp(cu))], axis=0),
               kk, cum, wl, r)
    rhs = each(lambda x, y, e: jnp.concatenate([to_blk(x * e), to_blk(y * e)], axis=0), kka, ke, e_neg)
    ends = each(lambda x, y, e: jnp.concatenate([to_blk(x * e), to_blk(y * e)], axis=0), kka, ke, e_end)
    vb = each(to_blk, v)
    cross = each(lambda a, b: lax.dot_general(a, b, NT_DIMS, preferred_element_type=F32), lhs, rhs)
    nab = each(lambda x: jnp.where(strict, x[:rows, :rows], 0.0), cross)
    aak = each(lambda x: jnp.where(strict, x[:rows, rows:], 0.0).astype(BF16), cross)
    rbk = each(lambda x: jnp.concatenate([jnp.where(incl, x[rows:, :rows], 0.0),
                                          jnp.where(incl, x[rows:, rows:], 0.0)], axis=1).astype(BF16), cross)
    inv = each(lambda x: eye + x, nab)
    pw = each(lambda x: x.astype(BF16), nab)
    for _ in range(max(1, int(math.ceil(math.log2(n))) - 1)):
        pw = each(lambda x: bdot(x, x).astype(BF16), pw)
        inv = each(lambda x, p: x + bdot(x.astype(BF16), p), inv, pw)
    inv = each(lambda x: x.astype(BF16), inv)
    akv = each(bdot, aak, vb)
    bonus = each(lambda rr, x, w, vv: head_sum(rr * x * w) * vv, r, ke, par(rk_ref), v)
    decay_end = each(jnp.exp, cum_last)
    ys = [None] * len(ids)
    for c in range(n_chunks):
        idx = [i for i, (cc, _) in enumerate(ids) if cc == c]
        take = lambda col: [col[i] for i in idx]
        s0 = [st_sc[ids[i][1]] for i in idx]
        ls = each(lambda a, s: lax.dot_general(a, s.astype(BF16), NT_DIMS, preferred_element_type=F32), take(lhs), s0)
        u = each(lambda m, x, y: bdot(m, (x[:rows] + y).astype(BF16)), take(inv), ls, take(akv))
        uv = each(lambda x, y: jnp.concatenate([x.astype(BF16), y], axis=0), u, take(vb))
        y = each(lambda x, m, z: from_blk(x[rows:] + bdot(m, z)), ls, take(rbk), uv)
        new = each(lambda s, d, z, e: s * d + lax.dot_general(z, e, TN_DIMS, preferred_element_type=F32),
                   s0, take(decay_end), uv, take(ends))
        for i, st, yy in zip(idx, new, y):
            st_sc[ids[i][1]] = st
            ys[i] = yy
    mu = each(lambda x: head_sum(x) * (1.0 / D_HEAD), ys)
    yc = each(lambda x, m: x - m, ys, mu)
    var = each(lambda x: head_sum(x * x) * (1.0 / D_HEAD), yc)
    yn = each(lambda x, s, w, b: x * lax.rsqrt(s + gn_eps) * w + b, yc, var, par(gw_ref), par(gb_ref))
    for sl, ln, x, b in zip(sls, lns, yn, bonus):
        o_ref[sl, ln] = ((x + b) * g_ref[sl, ln]).astype(o_ref.dtype)


def rwkv_scan(r, k, v, wl, a, g, k_k, k_a, r_k, gn_w, gn_b, batch, seq):
    t, width = r.shape
    hp = RWKV_HEADS_PER_STEP
    hw = hp * D_HEAD
    gw = _pick(width, (RWKV_GROUPS_PER_STEP * hw, hw))
    ch = RWKV_CHUNK
    tb = _pick(seq, (2 * ch, ch))
    nc = seq // tb
    xspec = pl.BlockSpec((tb, gw), lambda b, h, c: (b * nc + c, h))
    pspec = pl.BlockSpec((1, gw), lambda b, h, c: (0, h))
    vec = lambda p: p.reshape(1, width).astype(F32)
    return pl.pallas_call(
        functools.partial(_rwkv_scan_kernel, heads=hp, chunk=ch, gn_eps=D_GN_EPS),
        grid=(batch, width // gw, nc),
        in_specs=[xspec] * 6 + [pspec] * 5,
        out_specs=xspec,
        out_shape=jax.ShapeDtypeStruct((t, width), BF16),
        scratch_shapes=[pltpu.VMEM((gw // hw, hw, hw), F32)],
        compiler_params=_params("parallel", "parallel", "arbitrary"),
        name="rwkv_scan",
    )(r, k, v, wl, a, g, vec(k_k), vec(k_a), vec(r_k), vec(gn_w), vec(gn_b))


def _extract_top(work, dst_ref, count, want_rank=False):
    rank = jnp.full(work.shape, float(count), F32) if want_rank else None
    for r in range(count):
        m = jnp.max(work, axis=0, keepdims=True)
        dst_ref[r:r + 1, :] = m
        hit = work >= m
        if want_rank:
            rank = jnp.where(hit, float(r), rank)
        work = jnp.where(hit, -jnp.inf, work)
    return rank


def _peer_topk_kernel(q_ref, keys_ref, n1_ref, f1_ref, rk2_ref, e2_ref, a1_sc, a2_sc, cand_sc, top_sc):
    kk = P_TOPK
    for h in range(P_HEADS):
        sc = []
        for half in range(2):
            qh = q_ref[:, (2 * h + half) * LANES:(2 * h + half + 1) * LANES]
            sc.append(lax.dot_general(keys_ref[h, half], qh, NT_DIMS, precision=HIGHEST,
                                      preferred_element_type=F32))
        _extract_top(sc[0], a1_sc, kk)
        rank2 = _extract_top(sc[1], a2_sc, kk, want_rank=True)
        cand_sc[...] = jnp.full(cand_sc.shape, -jnp.inf, F32)
        off = 0
        for i in range(kk):
            nj = kk // (i + 1)
            cand_sc[off:off + nj, :] = a1_sc[i:i + 1, :] + a2_sc[0:nj, :]
            off += nj
        _extract_top(cand_sc[...], top_sc, kk)
        top = top_sc[...]
        best = top[0:1, :]
        tau = top[kk - 1:kk, :]
        zsum = jnp.sum(jnp.exp(top - best), axis=0, keepdims=True)
        count = jnp.zeros(sc[0].shape, F32)
        for j in range(kk):
            count = count + jnp.where(sc[0] + a2_sc[j:j + 1, :] >= tau, 1.0, 0.0)
        n1_ref[h] = count
        rk2_ref[h] = rank2.astype(BF16)
        f1_ref[h] = jnp.exp(sc[0] - a1_sc[0:1, :]) / zsum
        e2_ref[h] = jnp.exp(sc[1] - a2_sc[0:1, :]).astype(BF16)


def peer_topk(q, keys):
    t = q.shape[0]
    tt = _pick(t, (256, 128))
    n_cand = sum(P_TOPK // (i + 1) for i in range(P_TOPK))
    n_cand = -(-n_cand // 8) * 8
    big = pl.BlockSpec((P_HEADS, P_NKEYS, tt), lambda i: (0, 0, i))
    big_shape = jax.ShapeDtypeStruct((P_HEADS, P_NKEYS, t), F32)
    return pl.pallas_call(
        _peer_topk_kernel,
        grid=(t // tt,),
        in_specs=[pl.BlockSpec((tt, q.shape[1]), lambda i: (i, 0)),
                  pl.BlockSpec(keys.shape, lambda i: (0, 0, 0, 0))],
        out_specs=[big, big, big, big],
        out_shape=[big_shape, big_shape, big_shape.update(dtype=BF16), big_shape.update(dtype=BF16)],
        scratch_shapes=[pltpu.VMEM((P_TOPK, tt), F32), pltpu.VMEM((P_TOPK, tt), F32),
                        pltpu.VMEM((n_cand, tt), F32), pltpu.VMEM((P_TOPK, tt), F32)],
        compiler_params=_params("parallel"),
        name="peer_topk",
    )(q, keys)


def _peer_dense_kernel(sc_ref, x_ref, u_ref, v_ref, n1_ref, f1_ref, rk2_ref, e2_ref, o_ref, *, rows, chains):
    @pl.when(pl.program_id(1) == 0)
    def _():
        o_ref[...] = jnp.zeros(o_ref.shape, F32)

    tc = x_ref.shape[0] // chains
    hr = rows // 2
    dh = o_ref.shape[1] // 2
    tok = lambda ch: slice(ch * tc, (ch + 1) * tc)

    def activation(ch, half):
        es = slice(half * hr * P_NKEYS, (half + 1) * hr * P_NKEYS)
        return lax.dot_general(u_ref[es, :], x_ref[tok(ch), :], NT_DIMS, preferred_element_type=F32)

    def gated(ch, half, act):
        ts = tok(ch)
        sc = sc_ref[0]
        act = (act * (0.5 * sc)) * (1.0 + lax.erf(act * (sc * 2.0 ** -0.5)))
        act = act.astype(BF16)
        pk = P_NKEYS // BF16_ROWS
        parts = []
        for r in range(hr):
            row = half * hr + r
            coef = None
            for h in range(P_HEADS):
                n_row = jnp.broadcast_to(n1_ref[h, row:row + 1, ts], (BF16_ROWS, tc)).astype(BF16)
                f_row = jnp.broadcast_to(f1_ref[h, row:row + 1, ts], (BF16_ROWS, tc)).astype(BF16)
                sel = rk2_ref[h, :, :, ts] < n_row[None]
                c = jnp.where(sel, e2_ref[h, :, :, ts] * f_row[None], jnp.zeros((), BF16))
                coef = c if coef is None else coef + c
            a3 = act[r * P_NKEYS:(r + 1) * P_NKEYS, :].reshape(pk, BF16_ROWS, tc)
            parts.append((coef * a3).reshape(P_NKEYS, tc))
        return parts

    def quantise(parts):
        w = jnp.concatenate(parts, axis=0)
        amax = jnp.max(jnp.abs(w), axis=0, keepdims=True).astype(F32)
        k = jnp.minimum(jnp.floor(jnp.log2(FP8_TARGET_MAX / jnp.maximum(amax, 1e-30))), 100.0)
        wt = (w * jnp.exp2(k).astype(BF16)).T.astype(FP8)
        undo = jnp.transpose(jnp.broadcast_to(jnp.exp2(-k) * sc_ref[1], (LANES, tc)))[:, 0:1]
        return wt, undo

    def project(ch, half, wq):
        wt, undo = wq
        cs = slice(half * dh, (half + 1) * dh)
        o_ref[tok(ch), cs] += jnp.dot(wt, v_ref[:, cs], preferred_element_type=F32) * undo

    acts = [activation(0, 0), activation(0, 1)]
    w_prev = None
    for ch in range(chains):
        nxt, parts = [], []
        for half in range(2):
            parts += gated(ch, half, acts[half])
            if ch + 1 < chains:
                nxt.append(activation(ch + 1, half))
            if w_prev is not None:
                project(ch - 1, half, w_prev)
        w_prev = quantise(parts)
        acts = nxt
    project(chains - 1, 0, w_prev)
    project(chains - 1, 1, w_prev)


def peer_dense(x, u, v, act_scale, n1, f1, rk2, e2):
    t, d = x.shape
    n = u.shape[0]
    tt = _pick(t, (1024, 512, 256, 128))
    chains = 2 if tt % 256 == 0 else 1
    rows = 8
    tn = rows * P_NKEYS
    n1r = n1.reshape(P_HEADS, P_NKEYS // rows, rows, t)
    f1r = f1.reshape(P_HEADS, P_NKEYS // rows, rows, t)
    pk = P_NKEYS // BF16_ROWS
    rk2 = rk2.reshape(P_HEADS, pk, BF16_ROWS, t)
    e2 = e2.reshape(P_HEADS, pk, BF16_ROWS, t)
    once = pl.Buffered(1)
    rowspec = pl.BlockSpec((P_HEADS, None, rows, tt), lambda i, e: (0, e, 0, i))
    fullspec = pl.BlockSpec((P_HEADS, pk, BF16_ROWS, tt), lambda i, e: (0, 0, 0, i), pipeline_mode=once)
    return pl.pallas_call(
        functools.partial(_peer_dense_kernel, rows=rows, chains=chains),
        grid=(t // tt, n // tn),
        in_specs=[pl.BlockSpec(memory_space=pltpu.SMEM),
                  pl.BlockSpec((tt, d), lambda i, e: (i, 0), pipeline_mode=once),
                  pl.BlockSpec((tn, d), lambda i, e: (e, 0)),
                  pl.BlockSpec((tn, d), lambda i, e: (e, 0)),
                  rowspec, rowspec, fullspec, fullspec],
        out_specs=pl.BlockSpec((tt, d), lambda i, e: (i, 0), pipeline_mode=once),
        out_shape=jax.ShapeDtypeStruct((t, d), F32),
        compiler_params=_params("parallel", "arbitrary"),
        name="peer_dense",
    )(act_scale.astype(F32), x, u, v, n1r, f1r, rk2, e2)


def _cast_kernel(sc_ref, x_ref, o_ref):
    o_ref[...] = (x_ref[...].astype(F32) * sc_ref[0]).astype(o_ref.dtype)


def _stacked_spec(a, layer, tr):
    if layer is None:
        return pl.BlockSpec((tr, a.shape[-1]), lambda i: (i, 0))
    return pl.BlockSpec((None, tr, a.shape[-1]), lambda i: (layer, i, 0))


def scaled_cast(a, dtype, scale=None, layer=None):
    r, c = a.shape[-2:]
    tr = _pick(r, (512, 256, 128, 64, 32))
    scale = jnp.ones((1,), F32) if scale is None else scale.reshape(1).astype(F32)
    return pl.pallas_call(
        _cast_kernel,
        grid=(r // tr,),
        in_specs=[pl.BlockSpec(memory_space=pltpu.SMEM), _stacked_spec(a, layer, tr)],
        out_specs=pl.BlockSpec((tr, c), lambda i: (i, 0)),
        out_shape=jax.ShapeDtypeStruct((r, c), dtype),
        compiler_params=_params("parallel"),
        name="scaled_cast",
    )(scale, a)


def _absmax_kernel(x_ref, o_ref):
    m = jnp.max(jnp.abs(x_ref[...].astype(F32)), axis=0, keepdims=True)
    part = m[:, 0:LANES]
    for g in range(1, m.shape[1] // LANES):
        part = jnp.maximum(part, m[:, g * LANES:(g + 1) * LANES])
    o_ref[...] = jnp.broadcast_to(part, o_ref.shape)


def absmax(a, layer=None):
    r, c = a.shape[-2:]
    tr = _pick(r, (512, 256, 128, 64, 32))
    part = pl.pallas_call(
        _absmax_kernel,
        grid=(r // tr,),
        in_specs=[_stacked_spec(a, layer, tr)],
        out_specs=pl.BlockSpec((8, LANES), lambda i: (i, 0)),
        out_shape=jax.ShapeDtypeStruct((r // tr * 8, LANES), F32),
        compiler_params=_params("parallel"),
        name="absmax",
    )(a)
    return jnp.max(part)


def _fp8_scaled(a, layer=None):
    k = jnp.floor(jnp.log2(FP8_TARGET_MAX / jnp.maximum(absmax(a, layer), 1e-30)))
    scale = jnp.exp2(k)
    return scaled_cast(a, FP8, scale, layer), 1.0 / scale


def norm_fp8_scale(g, d):
    bound = math.sqrt(d) * jnp.max(jnp.abs(g)).astype(F32)
    return jnp.exp2(jnp.floor(jnp.log2(FP8_TARGET_MAX / jnp.maximum(bound, 1e-30))))


def peer_ffn(hn, x8, inv_x, w_q, keys, u, v, layer):
    q = matmul([hn], [scaled_cast(w_q, BF16, layer=layer)], F32)
    n1, f1, rk2, e2 = peer_topk(q, keys)
    u8, inv_u = _fp8_scaled(u, layer)
    v8, inv_v = _fp8_scaled(v, layer)
    return peer_dense(x8, u8, v8, jnp.stack([inv_x * inv_u, inv_v]), n1, f1, rk2, e2)


def even_mixer(hn, w_in, w_out, lam_params, subln_gain, lam_init, w_gate, b_gate, gla_norm, batch, seq, residual=None):
    na = w_out.shape[0] // 2
    a_heads = na // A_V_DIM
    b_heads = na // B_V_DIM
    sizes = (2 * a_heads * A_QK_DIM, 2 * a_heads * A_QK_DIM, na, b_heads * B_K_DIM, b_heads * B_K_DIM, na,
             w_gate.shape[0], na)
    names = ("aq", "ak", "av", "bq", "bk", "bv", "lo", "og")
    start, off = {}, 0
    for nm, sz in zip(names, sizes):
        start[nm] = off
        off += sz
    w_main = jnp.concatenate([w_in[:, :start["lo"]], w_in[:, start["og"]:]], axis=1).astype(BF16)
    cols = dict(start)
    cols["og"] = start["lo"]
    w_lo = _pad_cols(w_in[:, start["lo"]:start["og"]], LANES).astype(BF16)
    p = matmul([hn], [w_main], BF16)
    p_lo = matmul([hn], [w_lo], F32)
    qk = rope(p, rope_tables(seq, A_ROT), seq, width=start["av"], col_block=0, half=A_ROT // 2)
    lp = lam_params.astype(F32)
    lam = jnp.exp(jnp.sum(lp[0] * lp[1])) - jnp.exp(jnp.sum(lp[2] * lp[3])) + lam_init
    o_a = diff_attention(qk, p, lam, subln_gain, lam_init, batch, seq, start["av"])
    w_gate_p = jnp.pad(w_gate.astype(F32), ((0, LANES - w_gate.shape[0]), (0, 0)))
    o_b = gla(p, p_lo, w_gate_p, b_gate.astype(F32), gla_norm.astype(F32), batch, seq, cols)
    return matmul([o_a, o_b], scaled_cast(w_out, BF16), F32, residual, stacked=True)


def odd_mixer(hn, w_in, w_out, q_norm, kv_norm, w_uq, w_ukv, mu, w0, w2, a0, a2, g2,
              k_k, k_a, r_k, gn_w, gn_b, batch, seq, residual=None):
    q_rank, kv_rank = q_norm.shape[0], kv_norm.shape[0]
    width = w0.shape[0]
    rw, ra, rg = w2.shape[0], a2.shape[0], g2.shape[0]
    c_cols = q_rank + kv_rank + C_ROPE
    heads = w_uq.shape[1] // (C_NOPE + C_ROPE)
    c_pad = -(-(q_rank + kv_rank + LANES) // 256) * 256
    w_c = _pad_cols(w_in[:, :c_cols], c_pad).astype(BF16)
    pad_l = lambda m: -(-m // LANES) * LANES
    d0 = c_cols
    segs, mus, off = [], [], d0
    for sz in (width, width, width, rw, ra, rg):
        segs.append(_pad_cols(w_in[:, off:off + sz], pad_l(sz)))
        mus.append(jnp.pad(mu[off - d0:off - d0 + sz], (0, pad_l(sz) - sz)))
        off += sz
    w_d = jnp.concatenate(segs, axis=1).astype(BF16)
    mu_d = jnp.concatenate(mus).astype(F32)
    pad_r = lambda w: jnp.pad(w.astype(F32), ((0, pad_l(w.shape[0]) - w.shape[0]), (0, 0)))

    p_c = matmul([hn], [w_c], F32)
    p_d = matmul([hn], [w_d], F32)

    cq = add_rmsnorm([p_c], q_norm, BF16, width=q_rank, col_block=0)
    ckv = add_rmsnorm([p_c], kv_norm, BF16, width=kv_rank, col_block=q_rank // kv_rank)
    wq3 = w_uq.reshape(q_rank, heads, C_NOPE + C_ROPE)
    wq_nope = wq3[:, :, :C_NOPE].reshape(q_rank, heads * C_NOPE)
    wq_rope = jnp.pad(wq3[:, :, C_NOPE:], ((0, 0), (0, 0), (0, LANES - C_ROPE))).reshape(q_rank, heads * LANES)
    q_all = matmul([cq], [jnp.concatenate([wq_nope, wq_rope], axis=1).astype(BF16)], BF16)
    wkv3 = w_ukv.reshape(kv_rank, heads, C_NOPE + C_V)
    w_kv = jnp.concatenate([wkv3[:, :, :C_NOPE].reshape(kv_rank, heads * C_NOPE),
                            wkv3[:, :, C_NOPE:].reshape(kv_rank, heads * C_V)], axis=1).astype(BF16)
    kv = matmul([ckv], [w_kv], BF16)
    tables = rope_tables(seq, C_ROPE)
    q_rope = rope(q_all, tables, seq, width=heads * LANES, col_block=1, half=C_ROPE // 2)
    k_rope = rope(p_c, tables, seq, width=LANES, col_block=(q_rank + kv_rank) // LANES, half=C_ROPE // 2)
    o_c = mla_attention(q_all, q_rope, kv, k_rope, batch, seq)

    r, k, v, wl, a, g = rwkv_prep(p_d, mu_d, w0.astype(F32), pad_r(w2), a0.astype(F32), pad_r(a2), pad_r(g2),
                                  batch, seq)
    o_d = rwkv_scan(r, k, v, wl, a, g, k_k, k_a, r_k, gn_w, gn_b, batch, seq)
    return matmul([o_c, o_d], scaled_cast(w_out, BF16), F32, residual, stacked=True)


def kernel(x, norm_mix, norm_ffn, norm_final, even_w_in, even_w_out, diff_lambda, diff_subln, gla_w_gate, gla_b_gate, gla_norm, odd_w_in, odd_w_out, mla_q_norm, mla_kv_norm, mla_w_uq, mla_w_ukv, rwkv_mu, rwkv_w0, rwkv_w2, rwkv_a0, rwkv_a2, rwkv_g2, rwkv_k_k, rwkv_k_a, rwkv_r_k, rwkv_gn_w, rwkv_gn_b, peer_w_q, peer_keys, peer_u, peer_v):
    batch, seq, d = x.shape
    depth = norm_mix.shape[0]
    h = x.reshape(batch * seq, d)
    pending = []
    for layer in range(depth):
        j = layer // 2
        if pending:
            h, hn = add_rmsnorm([h] + pending, norm_mix[layer], BF16, emit_sum=True)
        else:
            hn = add_rmsnorm([h], norm_mix[layer], BF16)
        if layer % 2 == 0:
            lam_init = 0.8 - 0.6 * math.exp(-0.3 * layer)
            h = even_mixer(hn, even_w_in[j], even_w_out[j], diff_lambda[j], diff_subln[j], lam_init,
                           gla_w_gate[j], gla_b_gate[j], gla_norm[j], batch, seq, residual=h)
        else:
            h = odd_mixer(hn, odd_w_in[j], odd_w_out[j], mla_q_norm[j], mla_kv_norm[j], mla_w_uq[j],
                          mla_w_ukv[j], rwkv_mu[j], rwkv_w0[j], rwkv_w2[j], rwkv_a0[j], rwkv_a2[j],
                          rwkv_g2[j], rwkv_k_k[j], rwkv_k_a[j], rwkv_r_k[j], rwkv_gn_w[j], rwkv_gn_b[j],
                          batch, seq, residual=h)
        x_scale = norm_fp8_scale(norm_ffn[layer], d)
        hn2, x8 = add_rmsnorm([h], norm_ffn[layer], BF16, fp8_scale=x_scale)
        pending = [peer_ffn(hn2, x8, 1.0 / x_scale, peer_w_q, peer_keys[layer], peer_u, peer_v, layer)]
    out = add_rmsnorm([h] + pending, norm_final, F32)
    return out.reshape(batch, seq, d).astype(x.dtype)
```
